```python
import math
import jax, jax.numpy as jnp
from jax import lax
import numpy as np

D_MODEL = 1024
BATCH = 2
SEQ = 16384
DEPTH = 4
DEC_BATCH = 8
DEC_SEQ = 64
PAST_LEN = 4096

CHUNK = 64
HEAD_DIM = 64
A_HEADS = 8
A_BAND_CHUNKS = 8
A_PAST = A_BAND_CHUNKS * CHUNK
REL_CLIP = 128
B_HEADS = 4
B_VDIM = 2 * HEAD_DIM
ROT_DIM = HEAD_DIM // 4
ROPE_THETA = 500000.0
C_HEADS = 4
C_DK = 64
C_DV = 128
C_GATE_RANK = 16
C_TAU = 16.0
M_HEADS = 4
N_MEM = 256
N_BRANCH = 4
D_FF = 2816
N_EXPERTS = 8
TOP_K = 2
D_FF_EXPERT = 3584
N_DENSE = (DEPTH + 1) // 2
N_MOE = DEPTH // 2
MOE_BLOCK = 256
Q_BLOCK = 128
EPS = 1e-6
NEG_INF = -1e30

A_W = A_HEADS * HEAD_DIM
B_QK_W = B_HEADS * 2 * HEAD_DIM
B_V_W = B_HEADS * B_VDIM
C_QK_W = C_HEADS * C_DK
C_V_W = C_HEADS * C_DV
M_W = M_HEADS * HEAD_DIM
PROJ_SIZES = (A_W, A_W, A_W, B_QK_W, B_QK_W, B_V_W, C_QK_W, C_QK_W, C_V_W, C_GATE_RANK, C_V_W, M_W, N_BRANCH * D_MODEL)
PROJ_COLS = sum(PROJ_SIZES)
BRANCH_SIZES = (A_W, B_V_W, C_V_W, M_W)
BRANCH_ROWS = sum(BRANCH_SIZES)

kernel_name = 'hybrid_streaming_encoder_step'


def _split(x, sizes, axis=-1):
    idx = [int(v) for v in np.cumsum(sizes)[:-1]]
    return jnp.split(x, idx, axis=axis)


def _heads(t, n, d):
    return t.reshape(t.shape[:-1] + (n, d))


def rmsnorm(x, g):
    xf = x.astype(jnp.float32)
    y = xf * lax.rsqrt(jnp.mean(xf * xf, axis=-1, keepdims=True) + EPS)
    return (y * g.astype(jnp.float32)).astype(x.dtype)


def partial_rope(x, pos):
    half = ROT_DIM // 2
    inv_freq = ROPE_THETA ** (-jnp.arange(half, dtype=jnp.float32) / half)
    ang = pos.astype(jnp.float32)[:, None] * inv_freq[None, :]
    bshape = (pos.shape[0],) + (1,) * (x.ndim - 3) + (half,)
    cos = jnp.cos(ang).reshape(bshape)
    sin = jnp.sin(ang).reshape(bshape)
    xr = x[..., :ROT_DIM].astype(jnp.float32)
    x1, x2 = xr[..., :half], xr[..., half:]
    rot = jnp.concatenate([x1 * cos - x2 * sin, x2 * cos + x1 * sin], axis=-1)
    return jnp.concatenate([rot.astype(x.dtype), x[..., ROT_DIM:]], axis=-1)


def rel_bias(table, n_q, n_k, n_past):
    d = jnp.arange(n_q)[:, None] + n_past - jnp.arange(n_k)[None, :]
    d = jnp.clip(d, -REL_CLIP, REL_CLIP) + REL_CLIP
    return table[:, d].astype(jnp.float32)


def _band_prompt(q, k, v, table):
    b, s = q.shape[:2]
    nc = s // CHUNK
    band = A_PAST + CHUNK
    pad = ((0, 0), (A_PAST, 0), (0, 0), (0, 0))
    kp = jnp.pad(k, pad)
    vp = jnp.pad(v, pad)
    idx = (jnp.arange(nc) * CHUNK)[:, None] + jnp.arange(band)[None, :]
    kb = kp[:, idx]
    vb = vp[:, idx]
    qc = q.reshape(b, nc, CHUNK, A_HEADS, HEAD_DIM)
    sc = jnp.einsum('bcqhd,bckhd->bchqk', qc, kb, preferred_element_type=jnp.float32) * (HEAD_DIM ** -0.5)
    sc = sc + rel_bias(table, CHUNK, band, A_PAST)[None, None]
    sc = jnp.where((idx >= A_PAST)[None, :, None, None, :], sc, NEG_INF)
    p = jax.nn.softmax(sc, axis=-1).astype(v.dtype)
    o = jnp.einsum('bchqk,bckhd->bcqhd', p, vb)
    return o.reshape(b, s, A_HEADS, HEAD_DIM)


def _band_sample(q, k_new, v_new, k_cache, v_cache, table):
    n_past = k_cache.shape[1]
    k = jnp.concatenate([k_cache, k_new], axis=1)
    v = jnp.concatenate([v_cache, v_new], axis=1)
    sc = jnp.einsum('bqhd,bkhd->bhqk', q, k, preferred_element_type=jnp.float32) * (HEAD_DIM ** -0.5)
    sc = sc + rel_bias(table, q.shape[1], k.shape[1], n_past)[None]
    p = jax.nn.softmax(sc, axis=-1).astype(v.dtype)
    return jnp.einsum('bhqk,bkhd->bqhd', p, v)


def _diff_core(q, k, v, q_pos, k_pos, lam):
    sc = jnp.einsum('bqhjd,bkhjd->bhjqk', q, k, preferred_element_type=jnp.float32) * (HEAD_DIM ** -0.5)
    vis = (k_pos[None, :] // CHUNK) <= (q_pos[:, None] // CHUNK)
    sc = jnp.where(vis, sc, NEG_INF)
    p = jax.nn.softmax(sc, axis=-1)
    a = p[:, :, 0] - lam * p[:, :, 1]
    return jnp.einsum('bhqk,bkhe->bqhe', a.astype(v.dtype), v)


def _diff_prompt(q, k, v, lam):
    b, s = q.shape[:2]
    nb = s // Q_BLOCK
    pos = jnp.arange(s)
    qb = jnp.moveaxis(q.reshape((b, nb, Q_BLOCK) + q.shape[2:]), 1, 0)
    pb = pos.reshape(nb, Q_BLOCK)

    def block(args):
        qi, pi = args
        return _diff_core(qi, k, v, pi, pos, lam)

    o = lax.map(block, (qb, pb))
    return jnp.moveaxis(o, 0, 1).reshape(b, s, B_HEADS, B_VDIM)


def _gla_chunk(state, qc, kc, vc, gc):
    bcum = jnp.cumsum(gc, axis=1)
    c = qc.shape[1]
    tri = jnp.tril(jnp.ones((c, c), dtype=bool))
    dec = jnp.where(tri[None, :, :, None, None],
                    jnp.exp(jnp.minimum(bcum[:, :, None] - bcum[:, None, :], 0.0)), 0.0)
    att = jnp.einsum('bihd,bjhd,bijhd->bhij', qc, kc, dec)
    o = jnp.einsum('bhij,bjhe->bihe', att, vc) + jnp.einsum('bihd,bhde->bihe', qc * jnp.exp(bcum), state)
    b_last = bcum[:, -1]
    new_state = jnp.exp(b_last)[..., None] * state + jnp.einsum(
        'bjhd,bjhe->bhde', kc * jnp.exp(b_last[:, None] - bcum), vc)
    return new_state, o


def _gla_prompt(q, k, v, g):
    b, s = q.shape[:2]
    nc = s // CHUNK

    def to_chunks(t):
        return jnp.moveaxis(t.reshape((b, nc, CHUNK) + t.shape[2:]), 1, 0)

    s0 = jnp.zeros((b, C_HEADS, C_DK, C_DV), jnp.float32)
    s_fin, o = lax.scan(lambda st, xs: _gla_chunk(st, *xs), s0,
                        (to_chunks(q), to_chunks(k), to_chunks(v), to_chunks(g)))
    return jnp.moveaxis(o, 0, 1).reshape(b, s, C_HEADS, C_DV), s_fin


def _mem_kv(mem, g, w_kv, kn):
    k, v = jnp.split(rmsnorm(mem, g) @ w_kv, 2, axis=-1)
    return rmsnorm(_heads(k, M_HEADS, HEAD_DIM), kn), _heads(v, M_HEADS, HEAD_DIM)


def _mem_attn(q, k, v):
    sc = jnp.einsum('bqhd,bkhd->bhqk', q, k, preferred_element_type=jnp.float32) * (HEAD_DIM ** -0.5)
    p = jax.nn.softmax(sc, axis=-1).astype(v.dtype)
    return jnp.einsum('bhqk,bkhd->bqhd', p, v)


def _branch_inputs(h, pos, w_in_l, a_qn, a_kn, b_qn, b_kn, c_wa, c_ba, m_qn):
    aq, ak, av, bq, bk, bv, cq, ck, cv, ca, cr, mq, gates = _split(h @ w_in_l, PROJ_SIZES)
    aq = rmsnorm(_heads(aq, A_HEADS, HEAD_DIM), a_qn)
    ak = rmsnorm(_heads(ak, A_HEADS, HEAD_DIM), a_kn)
    av = _heads(av, A_HEADS, HEAD_DIM)
    bq = partial_rope(rmsnorm(bq.reshape(bq.shape[:-1] + (B_HEADS, 2, HEAD_DIM)), b_qn), pos)
    bk = partial_rope(rmsnorm(bk.reshape(bk.shape[:-1] + (B_HEADS, 2, HEAD_DIM)), b_kn), pos)
    bv = _heads(bv, B_HEADS, B_VDIM)
    cq = _heads(cq, C_HEADS, C_DK).astype(jnp.float32) * (C_DK ** -0.5)
    ck = _heads(ck, C_HEADS, C_DK).astype(jnp.float32)
    cv = _heads(cv, C_HEADS, C_DV).astype(jnp.float32)
    cg = _heads(jax.nn.log_sigmoid((ca @ c_wa + c_ba).astype(jnp.float32)) / C_TAU, C_HEADS, C_DK)
    mq = rmsnorm(_heads(mq, M_HEADS, HEAD_DIM), m_qn)
    return aq, ak, av, bq, bk, bv, cq, ck, cv, cg, cr, mq, gates


def _merge(oa, ob, oc, cr, om, gates, b_sub, lam_init, c_on, w_br, w_o):
    dt = oa.dtype
    lead = oa.shape[:2]
    ob = rmsnorm(ob, b_sub) * (1.0 - lam_init)
    oc = rmsnorm(oc.astype(dt), c_on) * jax.nn.silu(_heads(cr, C_HEADS, C_DV))
    w_a, w_b, w_c, w_m = _split(w_br, BRANCH_SIZES, axis=0)
    g_a, g_b, g_c, g_m = jnp.split(jax.nn.sigmoid(gates.astype(jnp.float32)).astype(dt), N_BRANCH, axis=-1)
    y = (g_a * (oa.reshape(lead + (A_W,)) @ w_a) + g_b * (ob.reshape(lead + (B_V_W,)) @ w_b)
         + g_c * (oc.reshape(lead + (C_V_W,)) @ w_c) + g_m * (om.reshape(lead + (M_W,)) @ w_m))
    return y @ w_o


def _swiglu(h, w13, w2):
    a, b = jnp.split(h @ w13, 2, axis=-1)
    return (jax.nn.silu(a) * b) @ w2


def _moe(h, router, w13, w2):
    shp = h.shape
    xt = h.reshape(-1, shp[-1])
    n = xt.shape[0]
    n_assign = n * TOP_K
    logits = jnp.dot(xt, router, preferred_element_type=jnp.float32)
    top_v, top_i = lax.top_k(logits, TOP_K)
    wts = jax.nn.softmax(top_v, axis=-1)
    flat_e = top_i.reshape(-1)
    flat_tok = jnp.repeat(jnp.arange(n), TOP_K)
    order = jnp.argsort(flat_e)
    s_e = flat_e[order]
    s_tok = flat_tok[order]
    s_w = wts.reshape(-1)[order]
    counts = jnp.bincount(flat_e, length=N_EXPERTS)
    padded = (counts + MOE_BLOCK - 1) // MOE_BLOCK * MOE_BLOCK
    pad_end = jnp.cumsum(padded)
    pad_start = pad_end - padded
    start = jnp.cumsum(counts) - counts
    dest = pad_start[s_e] + jnp.arange(n_assign) - start[s_e]
    n_rows = -(-(n_assign + N_EXPERTS * MOE_BLOCK) // MOE_BLOCK) * MOE_BLOCK
    n_blocks = n_rows // MOE_BLOCK
    buf = jnp.zeros((n_rows, shp[-1]), h.dtype).at[dest].set(xt[s_tok])
    blk_e = jnp.minimum(jnp.searchsorted(pad_end, jnp.arange(n_blocks) * MOE_BLOCK, side='right'), N_EXPERTS - 1)

    def expert_block(args):
        xb, e = args
        return _swiglu(xb, w13[e], w2[e])

    yb = lax.map(expert_block, (buf.reshape(n_blocks, MOE_BLOCK, shp[-1]), blk_e)).reshape(n_rows, shp[-1])
    y = yb[dest] * s_w[:, None].astype(h.dtype)
    out = jnp.zeros_like(xt).at[s_tok].add(y)
    return out.reshape(shp)


def setup_inputs(seed: int = 0) -> dict:
    key = jax.random.key(seed)
    ks = iter(jax.random.split(key, 48))
    f32 = jnp.float32

    def nrm(shape, scale):
        return jax.random.normal(next(ks), shape, f32) * scale

    def gain(shape):
        return 1.0 + nrm(shape, 0.02)

    a_len = min(A_PAST, PAST_LEN)
    return {
        'x_prompt': nrm((BATCH, SEQ, D_MODEL), 1.0),
        'x_sample': nrm((DEC_BATCH, DEC_SEQ, D_MODEL), 1.0),
        'cache_a_k': nrm((DEPTH, DEC_BATCH, a_len, A_HEADS, HEAD_DIM), 1.0),
        'cache_a_v': nrm((DEPTH, DEC_BATCH, a_len, A_HEADS, HEAD_DIM), 1.0),
        'cache_b_k': nrm((DEPTH, DEC_BATCH, PAST_LEN, B_HEADS, 2, HEAD_DIM), 1.0),
        'cache_b_v': nrm((DEPTH, DEC_BATCH, PAST_LEN, B_HEADS, B_VDIM), 1.0),
        'state_c': nrm((DEPTH, DEC_BATCH, C_HEADS, C_DK, C_DV), 1.0),
        'cache_mem_k': nrm((DEPTH, DEC_BATCH, N_MEM, M_HEADS, HEAD_DIM), 1.0),
        'cache_mem_v': nrm((DEPTH, DEC_BATCH, N_MEM, M_HEADS, HEAD_DIM), 1.0),
        'mem_prompt': nrm((BATCH, N_MEM, D_MODEL), 1.0),
        'norm1_g': gain((DEPTH, D_MODEL)),
        'w_in': nrm((DEPTH, D_MODEL, PROJ_COLS), D_MODEL ** -0.5),
        'a_q_norm': gain((DEPTH, HEAD_DIM)),
        'a_k_norm': gain((DEPTH, HEAD_DIM)),
        'a_rel_bias': nrm((DEPTH, A_HEADS, 2 * REL_CLIP + 1), 0.5),
        'b_q_norm': gain((DEPTH, HEAD_DIM)),
        'b_k_norm': gain((DEPTH, HEAD_DIM)),
        'b_lambda': nrm((DEPTH, 4, HEAD_DIM), 0.1),
        'b_subln': gain((DEPTH, B_VDIM)),
        'c_w_alpha': nrm((DEPTH, C_GATE_RANK, C_QK_W), C_GATE_RANK ** -0.5),
        'c_b_alpha': nrm((DEPTH, C_QK_W), 0.1),
        'c_out_norm': gain((DEPTH, C_DV)),
        'mem_norm': gain((DEPTH, D_MODEL)),
        'w_mem_kv': nrm((DEPTH, D_MODEL, 2 * M_W), D_MODEL ** -0.5),
        'm_q_norm': gain((DEPTH, HEAD_DIM)),
        'm_k_norm': gain((DEPTH, HEAD_DIM)),
        'w_branch': nrm((DEPTH, BRANCH_ROWS, D_MODEL), A_W ** -0.5),
        'w_out': nrm((DEPTH, D_MODEL, D_MODEL), 0.5 * D_MODEL ** -0.5),
        'norm2_g': gain((DEPTH, D_MODEL)),
        'ffn_w13': nrm((N_DENSE, D_MODEL, 2 * D_FF), D_MODEL ** -0.5),
        'ffn_w2': nrm((N_DENSE, D_FF, D_MODEL), 0.5 * D_FF ** -0.5),
        'moe_router': nrm((N_MOE, D_MODEL, N_EXPERTS), D_MODEL ** -0.5),
        'moe_w13': nrm((N_MOE, N_EXPERTS, D_MODEL, 2 * D_FF_EXPERT), D_MODEL ** -0.5),
        'moe_w2': nrm((N_MOE, N_EXPERTS, D_FF_EXPERT, D_MODEL), 0.5 * D_FF_EXPERT ** -0.5),
    }


def reference(x_prompt, x_sample, cache_a_k, cache_a_v, cache_b_k, cache_b_v, state_c,
              cache_mem_k, cache_mem_v, mem_prompt,
              norm1_g, w_in, a_q_norm, a_k_norm, a_rel_bias, b_q_norm, b_k_norm, b_lambda, b_subln,
              c_w_alpha, c_b_alpha, c_out_norm, mem_norm, w_mem_kv, m_q_norm, m_k_norm,
              w_branch, w_out, norm2_g, ffn_w13, ffn_w2, moe_router, moe_w13, moe_w2):
    xp, xs = x_prompt, x_sample
    seq = xp.shape[1]
    t_new = xs.shape[1]
    past = cache_b_k.shape[2]
    pos_p = jnp.arange(seq)
    pos_s = past + jnp.arange(t_new)
    pos_bk = jnp.arange(past + t_new)
    a_keep = min(A_PAST, seq)

    a_kp, a_vp, a_ks, a_vs = [], [], [], []
    b_kp, b_vp, b_ks, b_vs = [], [], [], []
    c_sp, c_ss, m_kp, m_vp = [], [], [], []

    for l in range(DEPTH):
        lam_init = 0.8 - 0.6 * math.exp(-0.3 * l)
        lq1, lk1, lq2, lk2 = b_lambda[l].astype(jnp.float32)
        lam = jnp.exp(jnp.sum(lq1 * lk1)) - jnp.exp(jnp.sum(lq2 * lk2)) + lam_init
        wl = (w_in[l], a_q_norm[l], a_k_norm[l], b_q_norm[l], b_k_norm[l], c_w_alpha[l], c_b_alpha[l], m_q_norm[l])
        ml = (b_subln[l], lam_init, c_out_norm[l], w_branch[l], w_out[l])

        aq, ak, av, bq, bk, bv, cq, ck, cv, cg, cr, mq, gates = _branch_inputs(rmsnorm(xp, norm1_g[l]), pos_p, *wl)
        oa = _band_prompt(aq, ak, av, a_rel_bias[l])
        ob = _diff_prompt(bq, bk, bv, lam)
        oc, c_fin = _gla_prompt(cq, ck, cv, cg)
        mk, mv = _mem_kv(mem_prompt, mem_norm[l], w_mem_kv[l], m_k_norm[l])
        om = _mem_attn(mq, mk, mv)
        xp = xp + _merge(oa, ob, oc, cr, om, gates, *ml)
        a_kp.append(ak[:, seq - a_keep:])
        a_vp.append(av[:, seq - a_keep:])
        b_kp.append(bk)
        b_vp.append(bv)
        c_sp.append(c_fin.astype(xp.dtype))
        m_kp.append(mk)
        m_vp.append(mv)

        aq, ak, av, bq, bk, bv, cq, ck, cv, cg, cr, mq, gates = _branch_inputs(rmsnorm(xs, norm1_g[l]), pos_s, *wl)
        oa = _band_sample(aq, ak, av, cache_a_k[l], cache_a_v[l], a_rel_bias[l])
        ob = _diff_core(bq, jnp.concatenate([cache_b_k[l], bk], axis=1),
                        jnp.concatenate([cache_b_v[l], bv], axis=1), pos_s, pos_bk, lam)
        c_new, oc = _gla_chunk(state_c[l].astype(jnp.float32), cq, ck, cv, cg)
        om = _mem_attn(mq, cache_mem_k[l], cache_mem_v[l])
        xs = xs + _merge(oa, ob, oc, cr, om, gates, *ml)
        a_ks.append(ak)
        a_vs.append(av)
        b_ks.append(bk)
        b_vs.append(bv)
        c_ss.append(c_new.astype(state_c.dtype))

        if l % 2 == 0:
            xp = xp + _swiglu(rmsnorm(xp, norm2_g[l]), ffn_w13[l // 2], ffn_w2[l // 2])
            xs = xs + _swiglu(rmsnorm(xs, norm2_g[l]), ffn_w13[l // 2], ffn_w2[l // 2])
        else:
            xp = xp + _moe(rmsnorm(xp, norm2_g[l]), moe_router[l // 2], moe_w13[l // 2], moe_w2[l // 2])
            xs = xs + _moe(rmsnorm(xs, norm2_g[l]), moe_router[l // 2], moe_w13[l // 2], moe_w2[l // 2])

    return (xp, xs,
            jnp.stack(a_kp), jnp.stack(a_vp), jnp.stack(a_ks), jnp.stack(a_vs),
            jnp.stack(b_kp), jnp.stack(b_vp), jnp.stack(b_ks), jnp.stack(b_vs),
            jnp.stack(c_sp), jnp.stack(c_ss),
            jnp.stack(m_kp), jnp.stack(m_vp))
```

```python
import functools
import math

import numpy as np
import jax
import jax.numpy as jnp
from jax import lax
from jax.experimental import pallas as pl
from jax.experimental.pallas import tpu as pltpu

F32 = jnp.float32
BF16 = jnp.bfloat16
I32 = jnp.int32

CHUNK = 64
HEAD_DIM = 64
A_HEADS = 8
A_BAND_CHUNKS = 8
A_PAST = A_BAND_CHUNKS * CHUNK
REL_CLIP = 128
B_HEADS = 4
B_VDIM = 2 * HEAD_DIM
ROT_DIM = HEAD_DIM // 4
ROPE_THETA = 500000.0
C_HEADS = 4
C_DK = 64
C_DV = 128
C_GATE_RANK = 16
C_TAU = 16.0
M_HEADS = 4
N_BRANCH = 4
N_EXPERTS = 8
TOP_K = 2
EPS = 1e-6
NEG_INF = -1e30

A_W = A_HEADS * HEAD_DIM
B_QK_W = B_HEADS * 2 * HEAD_DIM
B_V_W = B_HEADS * B_VDIM
C_QK_W = C_HEADS * C_DK
C_V_W = C_HEADS * C_DV
M_W = M_HEADS * HEAD_DIM

LANES = 128
VMEM_LIMIT_BYTES = 56 * 1024 * 1024

TOKEN_TILE = 256
A_Q_TILE = 256
B_Q_TILE = 512
B_K_TILE = 512
B_CACHE_TILE = 1024
C_TILE = 256
M_Q_TILE = 512
FF_CHUNK = 256
MOE_ROWS = 256
MOE_FF_CHUNK = 512
QK_SCALE = HEAD_DIM ** -0.5

_SEG = {}
_off = 0
for _name, _w in (("aq", A_W), ("ak", A_W), ("av", A_W), ("bq", B_QK_W), ("bk", B_QK_W), ("bv", B_V_W),
                  ("cq", C_QK_W), ("ck", C_QK_W), ("cv", C_V_W), ("cr", C_V_W), ("mq", M_W)):
    _SEG[_name] = (_off, _w)
    _off += _w
GATE_OFF = _off


def _cparams(*sem):
    return pltpu.CompilerParams(dimension_semantics=sem, vmem_limit_bytes=VMEM_LIMIT_BYTES)


def _const_spec(shape):
    nd = len(shape)
    return pl.BlockSpec(shape, lambda *_: (0,) * nd, pipeline_mode=pl.Buffered(1))


def _dot(a, b):
    return jnp.dot(a, b, preferred_element_type=F32)


def _dot_nt(a, b):
    return lax.dot_general(a, b, (((1,), (1,)), ((), ())), preferred_element_type=F32)


def _dot_tn(a, b):
    return lax.dot_general(a, b, (((0,), (0,)), ((), ())), preferred_element_type=F32)


def _rms(x, gain):
    return x * lax.rsqrt(jnp.mean(x * x, axis=-1, keepdims=True) + EPS) * gain


def _head_norm(y, bd_ref, gain):
    w = y.shape[-1]
    ms = _dot((y * y).astype(BF16), bd_ref[:w, :w])
    return y * lax.rsqrt(ms + EPS) * gain


def _sigmoid(x):
    return 1.0 / (1.0 + jnp.exp(-x))


def _inproj_body(x_ref, g1_ref, w_ref, bd_ref, aqn_ref, akn_ref, bqn_ref, bkn_ref, mqn_ref, cwa_ref, cba_ref,
                 rc_ref, rs1_ref, rs2_ref,
                 aq_o, ak_o, av_o, bq_o, bk_o, bkh_o, bv_o, bvh_o, cq_o, ck_o, cv_o, cg_o, cr_o, mq_o, gt_o):
    x = x_ref[...]
    h = _rms(x, g1_ref[...]).astype(BF16)
    d_model = x.shape[-1]

    def proj(name):
        lo, w = _SEG[name]
        return _dot(h, w_ref[:, lo:lo + w])

    def rope(y):
        reps = y.shape[-1] // LANES
        c = jnp.concatenate([rc_ref[...]] * reps, axis=1)
        s1 = jnp.concatenate([rs1_ref[...]] * reps, axis=1)
        s2 = jnp.concatenate([rs2_ref[...]] * reps, axis=1)
        half = ROT_DIM // 2
        return y * c + pltpu.roll(y, y.shape[-1] - half, 1) * s1 + pltpu.roll(y, half, 1) * s2

    aq_o[...] = (_head_norm(proj("aq"), bd_ref, aqn_ref[...]) * QK_SCALE).astype(BF16)
    ak_o[...] = _head_norm(proj("ak"), bd_ref, akn_ref[...])
    av_o[...] = proj("av")
    bq_o[...] = (rope(_head_norm(proj("bq"), bd_ref, bqn_ref[...])) * QK_SCALE).astype(BF16)
    bk = rope(_head_norm(proj("bk"), bd_ref, bkn_ref[...]))
    bk_o[...] = bk
    bkh_o[...] = bk.astype(BF16)
    bv = proj("bv")
    bv_o[...] = bv
    bvh_o[...] = bv.astype(BF16)
    cq_o[...] = proj("cq") * (C_DK ** -0.5)
    ck_o[...] = proj("ck")
    cv_o[...] = proj("cv").astype(BF16)
    cr_o[...] = proj("cr").astype(BF16)
    mq_o[...] = (_head_norm(proj("mq"), bd_ref, mqn_ref[...]) * QK_SCALE).astype(BF16)
    ca = _dot(h, w_ref[:, GATE_OFF + N_BRANCH * d_model:])
    z = _dot(ca.astype(BF16), cwa_ref[...]) + cba_ref[...]
    cg_o[...] = (jnp.minimum(z, 0.0) - jnp.log1p(jnp.exp(-jnp.abs(z)))) * (1.0 / C_TAU)
    for c in range(N_BRANCH):
        lo = GATE_OFF + c * d_model
        gt_o[:, c * d_model:(c + 1) * d_model] = _sigmoid(_dot(h, w_ref[:, lo:lo + d_model])).astype(BF16)


def _inproj(x, lw, rope_tabs):
    t, d = x.shape
    tm = TOKEN_TILE
    wcols = lw["w_in"].shape[1]
    row = lambda w: pl.BlockSpec((tm, w), lambda i: (i, 0))
    outs = [("aq", A_W, BF16), ("ak", A_W, F32), ("av", A_W, F32), ("bq", B_QK_W, BF16), ("bk", B_QK_W, F32),
            ("bkh", B_QK_W, BF16), ("bv", B_V_W, F32), ("bvh", B_V_W, BF16), ("cq", C_QK_W, F32),
            ("ck", C_QK_W, F32), ("cv", C_V_W, BF16), ("cg", C_QK_W, F32), ("cr", C_V_W, BF16),
            ("mq", M_W, BF16), ("gt", N_BRANCH * d, BF16)]
    res = pl.pallas_call(
        _inproj_body,
        grid=(t // tm,),
        in_specs=[row(d), _const_spec((1, d)), _const_spec((d, wcols)), _const_spec((A_W, A_W)),
                  _const_spec((1, A_W)), _const_spec((1, A_W)), _const_spec((1, B_QK_W)), _const_spec((1, B_QK_W)),
                  _const_spec((1, M_W)), _const_spec((LANES, C_QK_W)), _const_spec((1, C_QK_W)),
                  row(LANES), row(LANES), row(LANES)],
        out_specs=[row(w) for _, w, _ in outs],
        out_shape=[jax.ShapeDtypeStruct((t, w), dt) for _, w, dt in outs],
        compiler_params=_cparams("parallel"),
        name="inproj",
    )(x, lw["norm1_g"], lw["w_in"], lw["bd"], lw["a_qn"], lw["a_kn"], lw["b_qn"], lw["b_kn"], lw["m_qn"],
      lw["c_wa"], lw["c_ba"], *rope_tabs)
    return {n: r for (n, _, _), r in zip(outs, res)}


def _pair_attention(q2, k2, v2, bias_even, bias_odd):
    tq = q2.shape[0]
    lane = lax.broadcasted_iota(I32, q2.shape, 1)
    zero = jnp.zeros_like(q2)
    qq = jnp.concatenate([jnp.where(lane < HEAD_DIM, q2, zero), jnp.where(lane >= HEAD_DIM, q2, zero)], axis=0)
    s = _dot_nt(qq, k2)
    if bias_even is not None:
        s = s + jnp.concatenate([bias_even, bias_odd], axis=0)
    m = jnp.max(s, axis=-1, keepdims=True)
    p = jnp.exp(s - m)
    l = jnp.sum(p, axis=-1, keepdims=True)
    o = _dot(p.astype(BF16), v2) / l
    return jnp.where(lane < HEAD_DIM, o[:tq], o[tq:])


def _band_body(q_ref, k0_ref, k1_ref, k2_ref, v0_ref, v1_ref, v2_ref, bias_ref, o_ref):
    i = pl.program_id(1)
    tq = q_ref.shape[1]
    k = jnp.concatenate([k0_ref[0], k1_ref[0], k2_ref[0]], axis=0).astype(BF16)
    v = jnp.concatenate([v0_ref[0], v1_ref[0], v2_ref[0]], axis=0).astype(BF16)
    col = lax.broadcasted_iota(I32, (tq, 3 * tq), 1)
    pad = jnp.where(col >= 2 * tq - i * tq, 0.0, NEG_INF)
    for p in range(A_HEADS // 2):
        sl = slice(p * LANES, (p + 1) * LANES)
        o_ref[0, :, sl] = _pair_attention(q_ref[0, :, sl], k[:, sl], v[:, sl],
                                          bias_ref[2 * p] + pad, bias_ref[2 * p + 1] + pad).astype(o_ref.dtype)


def _band_bias(table, tq):
    r = np.arange(tq)[:, None]
    c = np.arange(3 * tq)[None, :]
    dist = r + 2 * tq - c
    qc = (r + 2 * tq) // CHUNK
    kc = c // CHUNK
    vis = (kc <= qc) & (kc >= qc - A_BAND_CHUNKS)
    idx = np.clip(dist, -REL_CLIP, REL_CLIP) + REL_CLIP
    return jnp.where(jnp.asarray(vis)[None], table[:, idx].astype(F32), NEG_INF)


def _band_prompt(aq, ak, av, table):
    b, s, w = aq.shape
    tq = A_Q_TILE
    assert A_PAST == 2 * tq and s % tq == 0
    qspec = pl.BlockSpec((1, tq, w), lambda bi, i: (bi, i, 0))
    kspec = lambda back: pl.BlockSpec((1, tq, w), lambda bi, i: (bi, jnp.maximum(i - back, 0), 0))
    return pl.pallas_call(
        _band_body,
        grid=(b, s // tq),
        in_specs=[qspec, kspec(2), kspec(1), kspec(0), kspec(2), kspec(1), kspec(0),
                  _const_spec((A_HEADS, tq, 3 * tq))],
        out_specs=qspec,
        out_shape=jax.ShapeDtypeStruct((b, s, w), BF16),
        compiler_params=_cparams("parallel", "parallel"),
        name="band_attention",
    )(aq, ak, ak, ak, av, av, av, _band_bias(table, tq))


def _dense_attn_body(*refs, heads, has_bias):
    if has_bias:
        q_ref, k_ref, v_ref, bias_ref, o_ref = refs
    else:
        q_ref, k_ref, v_ref, o_ref = refs
    k = k_ref[0].astype(BF16)
    v = v_ref[0].astype(BF16)
    for p in range(heads // 2):
        sl = slice(p * LANES, (p + 1) * LANES)
        be, bo = (bias_ref[2 * p], bias_ref[2 * p + 1]) if has_bias else (None, None)
        o_ref[0, :, sl] = _pair_attention(q_ref[0, :, sl], k[:, sl], v[:, sl], be, bo).astype(o_ref.dtype)


def _dense_attention(q, k, v, bias, tq):
    b, sq, w = q.shape
    sk = k.shape[1]
    heads = w // HEAD_DIM
    qspec = pl.BlockSpec((1, tq, w), lambda bi, i: (bi, i, 0))
    kspec = pl.BlockSpec((1, sk, w), lambda bi, i: (bi, 0, 0))
    in_specs = [qspec, kspec, kspec]
    args = [q, k, v]
    if bias is not None:
        in_specs.append(_const_spec(bias.shape))
        args.append(bias)
    return pl.pallas_call(
        functools.partial(_dense_attn_body, heads=heads, has_bias=bias is not None),
        grid=(b, sq // tq),
        in_specs=in_specs,
        out_specs=qspec,
        out_shape=jax.ShapeDtypeStruct((b, sq, w), BF16),
        compiler_params=_cparams("parallel", "parallel"),
        name="dense_attention",
    )(*args)


def _memkv_body(mem_ref, g_ref, w_ref, bd_ref, kn_ref, k_o, v_o):
    h = _rms(mem_ref[0], g_ref[...]).astype(BF16)
    kv = _dot(h, w_ref[...])
    k_o[0] = _head_norm(kv[:, :M_W], bd_ref, kn_ref[...])
    v_o[0] = kv[:, M_W:]


def _mem_kv(mem, lw):
    b, n, d = mem.shape
    spec = pl.BlockSpec((1, n, M_W), lambda bi: (bi, 0, 0))
    return pl.pallas_call(
        _memkv_body,
        grid=(b,),
        in_specs=[pl.BlockSpec((1, n, d), lambda bi: (bi, 0, 0)), _const_spec((1, d)), _const_spec((d, 2 * M_W)),
                  _const_spec((A_W, A_W)), _const_spec((1, M_W))],
        out_specs=[spec, spec],
        out_shape=[jax.ShapeDtypeStruct((b, n, M_W), F32)] * 2,
        compiler_params=_cparams("parallel"),
        name="mem_kv",
    )(mem, lw["mem_norm"], lw["w_mem_kv"], lw["bd"], lw["m_kn"])


def _diff_init(q_ref, qq_scr, m_scr, l_scr, acc_scr):
    q = q_ref[0]
    lane = lax.broadcasted_iota(I32, q.shape, 1)
    zero = jnp.zeros_like(q)
    tq = q.shape[0]
    qq_scr[:tq] = jnp.where(lane < HEAD_DIM, q, zero)
    qq_scr[tq:] = jnp.where(lane >= HEAD_DIM, q, zero)
    m_scr[...] = jnp.full(m_scr.shape, NEG_INF, F32)
    l_scr[...] = jnp.zeros(l_scr.shape, F32)
    acc_scr[...] = jnp.zeros(acc_scr.shape, F32)


def _diff_step(k, v, qq_scr, m_scr, l_scr, acc_scr, visible):
    s = _dot_nt(qq_scr[...], k)
    if visible is not None:
        s = jnp.where(visible, s, NEG_INF)
    m_prev = m_scr[...]
    m_new = jnp.maximum(m_prev, jnp.max(s, axis=-1, keepdims=True))
    alpha = jnp.exp(m_prev - m_new)
    p = jnp.exp(s - m_new)
    l_scr[...] = alpha * l_scr[...] + jnp.sum(p, axis=-1, keepdims=True)
    acc_scr[...] = alpha * acc_scr[...] + _dot(p.astype(BF16), v)
    m_scr[...] = m_new


def _diff_finish(lam_ref, sub_ref, o_ref, l_scr, acc_scr, lam_init):
    tq = o_ref.shape[1]
    lam_p = lam_ref[...]
    lam = (jnp.exp(jnp.sum(lam_p[0:1] * lam_p[1:2], axis=-1, keepdims=True))
           - jnp.exp(jnp.sum(lam_p[2:3] * lam_p[3:4], axis=-1, keepdims=True)) + lam_init)
    o = acc_scr[:tq] / l_scr[:tq] - lam * (acc_scr[tq:] / l_scr[tq:])
    o_ref[0] = (_rms(o, sub_ref[...]) * (1.0 - lam_init)).astype(o_ref.dtype)


def _diff_prompt_body(qi_ref, ki_ref, q_ref, k_ref, v_ref, lam_ref, sub_ref, o_ref,
                      qq_scr, m_scr, l_scr, acc_scr, *, lam_init):
    s_id = pl.program_id(2)
    qi = qi_ref[s_id]
    ki = ki_ref[s_id]
    tq = q_ref.shape[1]
    tk = k_ref.shape[1]

    @pl.when(ki == 0)
    def _():
        _diff_init(q_ref, qq_scr, m_scr, l_scr, acc_scr)

    row = lax.broadcasted_iota(I32, (2 * tq, tk), 0)
    row = jnp.where(row >= tq, row - tq, row)
    col = lax.broadcasted_iota(I32, (2 * tq, tk), 1)
    chunk_shift = int(math.log2(CHUNK))
    visible = ((ki * tk + col) >> chunk_shift) <= ((qi * tq + row) >> chunk_shift)
    _diff_step(k_ref[0], v_ref[0], qq_scr, m_scr, l_scr, acc_scr, visible)

    last_ki = ((qi + 1) * tq - 1) // tk

    @pl.when(ki == last_ki)
    def _():
        _diff_finish(lam_ref, sub_ref, o_ref, l_scr, acc_scr, lam_init)


def _diff_prompt(bq, bkh, bvh, lam_p, sub, lam_init):
    b, s, w = bq.shape
    tq = min(B_Q_TILE, s)
    tk = min(B_K_TILE, s)
    steps = [(qi, ki) for qi in range(s // tq) for ki in range(((qi + 1) * tq - 1) // tk + 1)]
    qi_tab = jnp.asarray([p[0] for p in steps], I32)
    ki_tab = jnp.asarray([p[1] for p in steps], I32)
    qspec = pl.BlockSpec((1, tq, LANES), lambda bi, h, st, qt, kt: (bi, qt[st], h))
    kspec = pl.BlockSpec((1, tk, LANES), lambda bi, h, st, qt, kt: (bi, kt[st], h))
    cspec = lambda shp: pl.BlockSpec(shp, lambda bi, h, st, qt, kt: (0, 0))
    return pl.pallas_call(
        functools.partial(_diff_prompt_body, lam_init=lam_init),
        grid_spec=pltpu.PrefetchScalarGridSpec(
            num_scalar_prefetch=2,
            grid=(b, B_HEADS, len(steps)),
            in_specs=[qspec, kspec, kspec, cspec((4, HEAD_DIM)), cspec((1, B_VDIM))],
            out_specs=qspec,
            scratch_shapes=[pltpu.VMEM((2 * tq, LANES), BF16), pltpu.VMEM((2 * tq, 1), F32),
                            pltpu.VMEM((2 * tq, 1), F32), pltpu.VMEM((2 * tq, B_VDIM), F32)]),
        out_shape=jax.ShapeDtypeStruct((b, s, w), BF16),
        compiler_params=_cparams("parallel", "parallel", "arbitrary"),
        name="diff_attention",
    )(qi_tab, ki_tab, bq, bkh, bvh, lam_p, sub)


def _diff_sample_body(q_ref, ck_ref, cv_ref, nk_ref, nv_ref, lam_ref, sub_ref, o_ref,
                      qq_scr, m_scr, l_scr, acc_scr, *, lam_init):
    t = pl.program_id(2)

    @pl.when(t == 0)
    def _():
        _diff_init(q_ref, qq_scr, m_scr, l_scr, acc_scr)

    _diff_step(ck_ref[0].astype(BF16), cv_ref[0].astype(BF16), qq_scr, m_scr, l_scr, acc_scr, None)

    @pl.when(t == pl.num_programs(2) - 1)
    def _():
        _diff_step(nk_ref[0], nv_ref[0], qq_scr, m_scr, l_scr, acc_scr, None)
        _diff_finish(lam_ref, sub_ref, o_ref, l_scr, acc_scr, lam_init)


def _diff_sample(bq, cache_k, cache_v, bkh, bvh, lam_p, sub, lam_init):
    b, tq, w = bq.shape
    past = cache_k.shape[1]
    tc = min(B_CACHE_TILE, past)
    assert past % tc == 0 and tq == CHUNK and past % CHUNK == 0
    qspec = pl.BlockSpec((1, tq, LANES), lambda bi, h, t: (bi, 0, h))
    cspec = pl.BlockSpec((1, tc, LANES), lambda bi, h, t: (bi, t, h))
    pspec = lambda shp: pl.BlockSpec(shp, lambda bi, h, t: (0, 0))
    return pl.pallas_call(
        functools.partial(_diff_sample_body, lam_init=lam_init),
        grid=(b, B_HEADS, past // tc),
        in_specs=[qspec, cspec, cspec, qspec, qspec, pspec((4, HEAD_DIM)), pspec((1, B_VDIM))],
        out_specs=qspec,
        scratch_shapes=[pltpu.VMEM((2 * tq, LANES), BF16), pltpu.VMEM((2 * tq, 1), F32),
                        pltpu.VMEM((2 * tq, 1), F32), pltpu.VMEM((2 * tq, B_VDIM), F32)],
        out_shape=jax.ShapeDtypeStruct((b, tq, w), BF16),
        compiler_params=_cparams("parallel", "parallel", "arbitrary"),
        name="diff_attention_sample",
    )(bq, cache_k, cache_v, bkh, bvh, lam_p, sub)


def _gla_decay_matrix(t):
    levels = int(math.log2(t))
    d = np.zeros((levels + 2, t, t), np.float32)
    u = np.arange(t)[None, :]
    i = np.arange(t)[:, None]
    for lv in range(levels):
        g = t >> lv
        base = i - i % g
        r = base + g // 2 - 1
        second = (i % g) >= g // 2
        d[lv] = np.where(second, (u > r) & (u <= i), (u > i) & (u <= r))
    d[levels] = u <= i
    d[levels + 1] = u > i
    return jnp.asarray(d.reshape((levels + 2) * t, t), BF16)


def _gla_body(q_ref, k_ref, v_ref, g_ref, r_ref, s0_ref, d_ref, on_ref, o_ref, sf_ref, s_scr):
    c = pl.program_id(1)
    t = q_ref.shape[1]
    levels = int(math.log2(t))

    @pl.when(c == 0)
    def _():
        s_scr[...] = s0_ref[0]

    q = q_ref[0]
    k = k_ref[0]
    g = g_ref[0]
    g_hi = g.astype(BF16)
    g_lo = (g - g_hi.astype(F32)).astype(BF16)
    g2 = jnp.concatenate([g_hi, g_lo], axis=1)

    def exponent(block):
        e = _dot(d_ref[block * t:(block + 1) * t, :], g2)
        return e[:, :C_QK_W] + e[:, C_QK_W:]

    lane = lax.broadcasted_iota(I32, (t, C_QK_W), 1)
    head_lanes = [(lane >= h * C_DK) & (lane < (h + 1) * C_DK) for h in range(C_HEADS)]
    tok = lax.broadcasted_iota(I32, (t, C_QK_W), 0)
    ri = lax.broadcasted_iota(I32, (t, t), 0)
    ci = lax.broadcasted_iota(I32, (t, t), 1)
    zero = jnp.zeros((t, C_QK_W), BF16)
    kb = k.astype(BF16)
    qb = q.astype(BF16)

    att = [jnp.where(ri == ci, _dot_nt(jnp.where(head_lanes[h], qb, zero), kb), 0.0) for h in range(C_HEADS)]
    for lv in range(levels):
        shift = levels - lv
        w = jnp.exp(exponent(lv))
        second = ((tok >> (shift - 1)) & 1) == 1
        ql = jnp.where(second, q * w, 0.0).astype(BF16)
        kl = jnp.where(second, 0.0, k * w).astype(BF16)
        same = (ri >> shift) == (ci >> shift)
        for h in range(C_HEADS):
            att[h] = att[h] + jnp.where(same, _dot_nt(jnp.where(head_lanes[h], ql, zero), kl), 0.0)

    state = s_scr[...]
    state_b = state.astype(BF16)
    q_in = (q * jnp.exp(exponent(levels))).astype(BF16)
    k_out = (k * jnp.exp(exponent(levels + 1))).astype(BF16)
    ones = jnp.ones((t, C_DV), BF16)
    decay = jnp.exp(_dot_tn(g_hi, ones) + _dot_tn(g_lo, ones))
    srow = lax.broadcasted_iota(I32, (C_QK_W, C_DV), 0)
    new_state = decay * state
    for h in range(C_HEADS):
        sl = slice(h * C_DV, (h + 1) * C_DV)
        v = v_ref[0, :, sl]
        o = _dot(att[h].astype(BF16), v) + _dot(jnp.where(head_lanes[h], q_in, zero), state_b)
        r = r_ref[0, :, sl].astype(F32)
        o_ref[0, :, sl] = (_rms(o, on_ref[...]) * (r * _sigmoid(r))).astype(o_ref.dtype)
        upd = _dot_tn(k_out, v)
        new_state = new_state + jnp.where((srow >= h * C_DK) & (srow < (h + 1) * C_DK), upd, 0.0)
    s_scr[...] = new_state

    @pl.when(c == pl.num_programs(1) - 1)
    def _():
        sf_ref[0] = new_state


def _gla(cq, ck, cv, cg, cr, state0, c_on, t):
    b, s, _ = cq.shape
    levels = int(math.log2(t))
    assert 1 << levels == t and s % t == 0
    tok = lambda w: pl.BlockSpec((1, t, w), lambda bi, c: (bi, c, 0))
    sspec = pl.BlockSpec((1, C_QK_W, C_DV), lambda bi, c: (bi, 0, 0))
    return pl.pallas_call(
        _gla_body,
        grid=(b, s // t),
        in_specs=[tok(C_QK_W), tok(C_QK_W), tok(C_V_W), tok(C_QK_W), tok(C_V_W), sspec,
                  _const_spec(((levels + 2) * t, t)), _const_spec((1, C_DV))],
        out_specs=[tok(C_V_W), sspec],
        out_shape=[jax.ShapeDtypeStruct((b, s, C_V_W), BF16), jax.ShapeDtypeStruct((b, C_QK_W, C_DV), F32)],
        scratch_shapes=[pltpu.VMEM((C_QK_W, C_DV), F32)],
        compiler_params=_cparams("parallel", "arbitrary"),
        name="gla",
    )(cq, ck, cv, cg, cr, state0, _gla_decay_matrix(t), c_on)


def _merge_residual(x_ref, oa_ref, ob_ref, oc_ref, om_ref, gt_ref, wa_ref, wb_ref, wc_ref, wm_ref, wo_ref):
    d = x_ref.shape[-1]
    gate = lambda c: gt_ref[:, c * d:(c + 1) * d].astype(F32)
    y = (gate(0) * _dot(oa_ref[...], wa_ref[...]) + gate(1) * _dot(ob_ref[...], wb_ref[...])
         + gate(2) * _dot(oc_ref[...], wc_ref[...]) + gate(3) * _dot(om_ref[...], wm_ref[...]))
    return x_ref[...] + _dot(y.astype(BF16), wo_ref[...])


def _merge_ffn_body(x_ref, oa_ref, ob_ref, oc_ref, om_ref, gt_ref, wa_ref, wb_ref, wc_ref, wm_ref, wo_ref,
                    g2_ref, w13_ref, w2_ref, o_ref):
    x1 = _merge_residual(x_ref, oa_ref, ob_ref, oc_ref, om_ref, gt_ref, wa_ref, wb_ref, wc_ref, wm_ref, wo_ref)
    h = _rms(x1, g2_ref[...]).astype(BF16)
    d_ff = w2_ref.shape[0]
    acc = x1
    for c in range(d_ff // FF_CHUNK):
        lo = c * FF_CHUNK
        a = _dot(h, w13_ref[:, lo:lo + FF_CHUNK])
        b = _dot(h, w13_ref[:, d_ff + lo:d_ff + lo + FF_CHUNK])
        acc = acc + _dot((a * _sigmoid(a) * b).astype(BF16), w2_ref[lo:lo + FF_CHUNK, :])
    o_ref[...] = acc


def _merge_router_body(x_ref, oa_ref, ob_ref, oc_ref, om_ref, gt_ref, wa_ref, wb_ref, wc_ref, wm_ref, wo_ref,
                       g2_ref, rh_ref, rl_ref, x1_o, h_o, idx_o, wt_o):
    x1 = _merge_residual(x_ref, oa_ref, ob_ref, oc_ref, om_ref, gt_ref, wa_ref, wb_ref, wc_ref, wm_ref, wo_ref)
    x1_o[...] = x1
    h = _rms(x1, g2_ref[...])
    h_o[...] = h
    h_hi = h.astype(BF16)
    h_lo = (h - h_hi.astype(F32)).astype(BF16)
    logits = _dot(h_hi, rh_ref[...]) + _dot(h_lo, rh_ref[...]) + _dot(h_hi, rl_ref[...])
    lane = lax.broadcasted_iota(I32, logits.shape, 1)
    lane_f = lane.astype(F32)
    neg = jnp.float32(-jnp.inf)
    lg = jnp.where(lane < N_EXPERTS, logits, neg)
    m1 = jnp.max(lg, axis=-1, keepdims=True)
    i1 = jnp.min(jnp.where(lg == m1, lane_f, float(LANES)), axis=-1, keepdims=True)
    lg2 = jnp.where(lane_f == i1, neg, lg)
    m2 = jnp.max(lg2, axis=-1, keepdims=True)
    i2 = jnp.min(jnp.where(lg2 == m2, lane_f, float(LANES)), axis=-1, keepdims=True)
    e = jnp.exp(m2 - m1)
    w1 = 1.0 / (1.0 + e)
    idx_o[...] = jnp.where(lane == 0, i1, jnp.where(lane == 1, i2, 0.0)).astype(I32)
    wt_o[...] = jnp.where(lane == 0, w1, jnp.where(lane == 1, e * w1, 0.0))


def _merge_specs(t, d):
    tm = TOKEN_TILE
    row = lambda w: pl.BlockSpec((tm, w), lambda i: (i, 0))
    specs = [row(d), row(A_W), row(B_V_W), row(C_V_W), row(M_W), row(N_BRANCH * d),
             _const_spec((A_W, d)), _const_spec((B_V_W, d)), _const_spec((C_V_W, d)), _const_spec((M_W, d)),
             _const_spec((d, d)), _const_spec((1, d))]
    return tm, row, specs


def _merge_args(x, br, lw):
    return (x, br["oa"], br["ob"], br["oc"], br["om"], br["gt"], lw["w_a"], lw["w_b"], lw["w_c"], lw["w_m"],
            lw["w_o"], lw["norm2_g"])


def _merge_ffn(x, br, lw):
    t, d = x.shape
    tm, row, specs = _merge_specs(t, d)
    d_ff = lw["ffn_w2"].shape[0]
    assert d_ff % FF_CHUNK == 0
    return pl.pallas_call(
        _merge_ffn_body,
        grid=(t // tm,),
        in_specs=specs + [_const_spec((d, 2 * d_ff)), _const_spec((d_ff, d))],
        out_specs=row(d),
        out_shape=jax.ShapeDtypeStruct((t, d), F32),
        compiler_params=_cparams("parallel"),
        name="merge_ffn",
    )(*_merge_args(x, br, lw), lw["ffn_w13"], lw["ffn_w2"])


def _merge_router(x, br, lw):
    t, d = x.shape
    tm, row, specs = _merge_specs(t, d)
    return pl.pallas_call(
        _merge_router_body,
        grid=(t // tm,),
        in_specs=specs + [_const_spec((d, LANES)), _const_spec((d, LANES))],
        out_specs=[row(d), row(d), row(LANES), row(LANES)],
        out_shape=[jax.ShapeDtypeStruct((t, d), F32), jax.ShapeDtypeStruct((t, d), F32),
                   jax.ShapeDtypeStruct((t, LANES), I32), jax.ShapeDtypeStruct((t, LANES), F32)],
        compiler_params=_cparams("parallel"),
        name="merge_router",
    )(*_merge_args(x, br, lw), lw["router_hi"], lw["router_lo"])


def _row_gather_copy(src_hbm, idx_ref, n, dst, sem):
    def issue(r, carry):
        pltpu.make_async_copy(src_hbm.at[pl.ds(idx_ref[0, 0, r], 1)], dst.at[pl.ds(r, 1)], sem).start()
        return carry
    lax.fori_loop(0, n, issue, 0, unroll=8)


def _row_gather_wait(dst, sem):
    pltpu.make_async_copy(dst, dst, sem).wait()


def _moe_body(blk_e_ref, cur_ref, nxt_ref, h_hbm, w13_ref, w2_ref, y_ref, xbuf, sem):
    i = pl.program_id(0)
    n = pl.num_programs(0)
    rows = xbuf.shape[1]
    slot = i % 2

    @pl.when(i == 0)
    def _():
        _row_gather_copy(h_hbm, cur_ref, rows, xbuf.at[0], sem.at[0])

    @pl.when(i + 1 < n)
    def _():
        _row_gather_copy(h_hbm, nxt_ref, rows, xbuf.at[1 - slot], sem.at[1 - slot])

    _row_gather_wait(xbuf.at[slot], sem.at[slot])
    x = xbuf[slot].astype(BF16)
    d_ff = w2_ref.shape[1]
    acc = jnp.zeros(y_ref.shape, F32)
    for c in range(d_ff // MOE_FF_CHUNK):
        lo = c * MOE_FF_CHUNK
        a = _dot(x, w13_ref[0, :, lo:lo + MOE_FF_CHUNK])
        b = _dot(x, w13_ref[0, :, d_ff + lo:d_ff + lo + MOE_FF_CHUNK])
        acc = acc + _dot((a * _sigmoid(a) * b).astype(BF16), w2_ref[0, lo:lo + MOE_FF_CHUNK, :])
    y_ref[...] = acc


def _moe_experts(h, row_tok, blk_e, w13, w2):
    t, d = h.shape
    n_blocks = blk_e.shape[0]
    rows = MOE_ROWS
    d_ff = w2.shape[1]
    assert d_ff % MOE_FF_CHUNK == 0
    tok3 = row_tok.reshape(n_blocks, 1, rows)
    smem = lambda f: pl.BlockSpec((1, 1, rows), f, memory_space=pltpu.SMEM)
    return pl.pallas_call(
        _moe_body,
        grid_spec=pltpu.PrefetchScalarGridSpec(
            num_scalar_prefetch=1,
            grid=(n_blocks,),
            in_specs=[smem(lambda i, be: (i, 0, 0)),
                      smem(lambda i, be: (jnp.minimum(i + 1, n_blocks - 1), 0, 0)),
                      pl.BlockSpec(memory_space=pl.ANY),
                      pl.BlockSpec((1, d, 2 * d_ff), lambda i, be: (be[i], 0, 0)),
                      pl.BlockSpec((1, d_ff, d), lambda i, be: (be[i], 0, 0))],
            out_specs=pl.BlockSpec((rows, d), lambda i, be: (i, 0)),
            scratch_shapes=[pltpu.VMEM((2, rows, d), F32), pltpu.SemaphoreType.DMA((2,))]),
        out_shape=jax.ShapeDtypeStruct((n_blocks * rows, d), F32),
        compiler_params=_cparams("arbitrary"),
        name="moe_experts",
    )(blk_e, tok3, tok3, h, w13, w2)


def _combine_body(cur_ref, nxt_ref, x_ref, wt_ref, y_hbm, o_ref, ybuf, sem):
    i = pl.program_id(0)
    n = pl.num_programs(0)
    tm = x_ref.shape[0]
    slot = i % 2

    @pl.when(i == 0)
    def _():
        _row_gather_copy(y_hbm, cur_ref, 2 * tm, ybuf.at[0], sem.at[0])

    @pl.when(i + 1 < n)
    def _():
        _row_gather_copy(y_hbm, nxt_ref, 2 * tm, ybuf.at[1 - slot], sem.at[1 - slot])

    _row_gather_wait(ybuf.at[slot], sem.at[slot])
    wt = wt_ref[...]
    o_ref[...] = x_ref[...] + wt[:, 0:1] * ybuf[slot, :tm] + wt[:, 1:2] * ybuf[slot, tm:]


def _moe_combine(x1, wt, y, dest):
    t, d = x1.shape
    tm = TOKEN_TILE
    n = t // tm
    dest3 = dest.reshape(n, tm, 2).transpose(0, 2, 1).reshape(n, 1, 2 * tm)
    smem = lambda f: pl.BlockSpec((1, 1, 2 * tm), f, memory_space=pltpu.SMEM)
    row = lambda w: pl.BlockSpec((tm, w), lambda i: (i, 0))
    return pl.pallas_call(
        _combine_body,
        grid=(n,),
        in_specs=[smem(lambda i: (i, 0, 0)), smem(lambda i: (jnp.minimum(i + 1, n - 1), 0, 0)),
                  row(d), row(LANES), pl.BlockSpec(memory_space=pl.ANY)],
        out_specs=row(d),
        out_shape=jax.ShapeDtypeStruct((t, d), F32),
        scratch_shapes=[pltpu.VMEM((2, 2 * tm, d), F32), pltpu.SemaphoreType.DMA((2,))],
        compiler_params=_cparams("arbitrary"),
        name="moe_combine",
    )(dest3, dest3, x1, wt, y)


def _moe(x1, h, idx, wt, w13, w2):
    t, d = h.shape
    top_i = idx[:, :TOP_K]
    flat_e = top_i.reshape(-1)
    onehot = (flat_e[:, None] == jnp.arange(N_EXPERTS, dtype=I32)[None, :]).astype(I32)
    csum = jnp.cumsum(onehot, axis=0)
    rank = jnp.take_along_axis(csum, flat_e[:, None], axis=1)[:, 0] - 1
    counts = csum[-1]
    padded = (counts + MOE_ROWS - 1) // MOE_ROWS * MOE_ROWS
    pad_end = jnp.cumsum(padded)
    pad_start = pad_end - padded
    dest = pad_start[flat_e] + rank
    n_blocks = -(-(t * TOP_K + N_EXPERTS * (MOE_ROWS - 1)) // MOE_ROWS)
    n_rows = n_blocks * MOE_ROWS
    flat_tok = jnp.arange(t * TOP_K, dtype=I32) // TOP_K
    row_tok = jnp.zeros((n_rows,), I32).at[dest].set(flat_tok)
    blk_e = jnp.minimum(jnp.searchsorted(pad_end, jnp.arange(n_blocks, dtype=I32) * MOE_ROWS, side="right"),
                        N_EXPERTS - 1).astype(I32)
    y = _moe_experts(h, row_tok, blk_e, w13, w2)
    return _moe_combine(x1, wt, y, dest.reshape(t, TOP_K).astype(I32))


def _rope_tables(pos):
    half = ROT_DIM // 2
    inv_freq = ROPE_THETA ** (-jnp.arange(half, dtype=F32) / half)
    ang = pos.astype(F32)[:, None] * inv_freq[None, :]
    cos, sin = jnp.cos(ang), jnp.sin(ang)
    n = pos.shape[0]
    ones = jnp.ones((n, HEAD_DIM - ROT_DIM), F32)
    zeros8 = jnp.zeros((n, half), F32)
    zeros = jnp.zeros((n, HEAD_DIM - ROT_DIM), F32)
    c = jnp.concatenate([cos, cos, ones], axis=1)
    s1 = jnp.concatenate([-sin, zeros8, zeros], axis=1)
    s2 = jnp.concatenate([zeros8, sin, zeros], axis=1)
    return tuple(jnp.tile(a, (1, LANES // HEAD_DIM)) for a in (c, s1, s2))


def _layer_weights(l, p):
    d = p["w_in"].shape[1]
    w_in = p["w_in"][l]
    sizes = (A_W, A_W, A_W, B_QK_W, B_QK_W, B_V_W, C_QK_W, C_QK_W, C_V_W, C_GATE_RANK, C_V_W, M_W, N_BRANCH * d)
    offs = np.concatenate([[0], np.cumsum(sizes)])
    seg = lambda j: w_in[:, int(offs[j]):int(offs[j + 1])]
    ca_pad = jnp.pad(seg(9), ((0, 0), (0, LANES - C_GATE_RANK)))
    w_all = jnp.concatenate([seg(j) for j in (0, 1, 2, 3, 4, 5, 6, 7, 8, 10, 11, 12)] + [ca_pad], axis=1)
    blk = np.kron(np.eye(A_W // HEAD_DIM, dtype=np.float32), np.full((HEAD_DIM, HEAD_DIM), 1.0 / HEAD_DIM, np.float32))
    tile = lambda g, n: jnp.tile(g, n)[None, :].astype(F32)
    w_br = p["w_branch"][l].astype(BF16)
    o1, o2, o3 = A_W, A_W + B_V_W, A_W + B_V_W + C_V_W
    lw = {
        "norm1_g": p["norm1_g"][l][None, :], "w_in": w_all.astype(BF16), "bd": jnp.asarray(blk, BF16),
        "a_qn": tile(p["a_q_norm"][l], A_HEADS), "a_kn": tile(p["a_k_norm"][l], A_HEADS),
        "b_qn": tile(p["b_q_norm"][l], 2 * B_HEADS), "b_kn": tile(p["b_k_norm"][l], 2 * B_HEADS),
        "m_qn": tile(p["m_q_norm"][l], M_HEADS), "m_kn": tile(p["m_k_norm"][l], M_HEADS),
        "c_wa": jnp.pad(p["c_w_alpha"][l], ((0, LANES - C_GATE_RANK), (0, 0))).astype(BF16),
        "c_ba": p["c_b_alpha"][l][None, :],
        "mem_norm": p["mem_norm"][l][None, :], "w_mem_kv": p["w_mem_kv"][l].astype(BF16),
        "w_a": w_br[:o1], "w_b": w_br[o1:o2], "w_c": w_br[o2:o3], "w_m": w_br[o3:],
        "w_o": p["w_out"][l].astype(BF16), "norm2_g": p["norm2_g"][l][None, :],
        "b_lambda": p["b_lambda"][l], "b_subln": p["b_subln"][l][None, :], "c_on": p["c_out_norm"][l][None, :],
        "a_rel_bias": p["a_rel_bias"][l],
    }
    if l % 2 == 0:
        lw["ffn_w13"] = p["ffn_w13"][l // 2].astype(BF16)
        lw["ffn_w2"] = p["ffn_w2"][l // 2].astype(BF16)
    else:
        r = jnp.pad(p["moe_router"][l // 2], ((0, 0), (0, LANES - N_EXPERTS)))
        r_hi = r.astype(BF16)
        lw["router_hi"] = r_hi
        lw["router_lo"] = (r - r_hi.astype(F32)).astype(BF16)
        lw["moe_w13"] = p["moe_w13"][l // 2].astype(BF16)
        lw["moe_w2"] = p["moe_w2"][l // 2].astype(BF16)
    return lw


def _channel_mixer(l, x, br, lw):
    if l % 2 == 0:
        return _merge_ffn(x, br, lw)
    x1, h, idx, wt = _merge_router(x, br, lw)
    return _moe(x1, h, idx, wt, lw["moe_w13"], lw["moe_w2"])


def kernel(x_prompt, x_sample, cache_a_k, cache_a_v, cache_b_k, cache_b_v, state_c, cache_mem_k, cache_mem_v,
           mem_prompt, norm1_g, w_in, a_q_norm, a_k_norm, a_rel_bias, b_q_norm, b_k_norm, b_lambda, b_subln,
           c_w_alpha, c_b_alpha, c_out_norm, mem_norm, w_mem_kv, m_q_norm, m_k_norm, w_branch, w_out, norm2_g,
           ffn_w13, ffn_w2, moe_router, moe_w13, moe_w2):
    p = dict(norm1_g=norm1_g, w_in=w_in, a_q_norm=a_q_norm, a_k_norm=a_k_norm, a_rel_bias=a_rel_bias,
             b_q_norm=b_q_norm, b_k_norm=b_k_norm, b_lambda=b_lambda, b_subln=b_subln, c_w_alpha=c_w_alpha,
             c_b_alpha=c_b_alpha, c_out_norm=c_out_norm, mem_norm=mem_norm, w_mem_kv=w_mem_kv, m_q_norm=m_q_norm,
             m_k_norm=m_k_norm, w_branch=w_branch, w_out=w_out, norm2_g=norm2_g, ffn_w13=ffn_w13, ffn_w2=ffn_w2,
             moe_router=moe_router, moe_w13=moe_w13, moe_w2=moe_w2)
    depth = w_in.shape[0]
    nb, seq, d = x_prompt.shape
    sb, t_new, _ = x_sample.shape
    past = cache_b_k.shape[2]
    n_mem = mem_prompt.shape[1]
    a_keep = min(A_PAST, seq)
    assert t_new == CHUNK and cache_a_k.shape[2] == A_PAST and seq % TOKEN_TILE == 0 and (sb * t_new) % TOKEN_TILE == 0

    rope_p = _rope_tables(jnp.tile(jnp.arange(seq), nb))
    rope_s = _rope_tables(jnp.tile(past + jnp.arange(t_new), sb))
    xp = x_prompt.reshape(nb * seq, d)
    xs = x_sample.reshape(sb * t_new, d)
    c_tile = min(C_TILE, seq)
    outs = {k: [] for k in ("a_kp", "a_vp", "a_ks", "a_vs", "b_kp", "b_vp", "b_ks", "b_vs", "c_sp", "c_ss",
                            "m_kp", "m_vp")}

    for l in range(depth):
        lam_init = 0.8 - 0.6 * math.exp(-0.3 * l)
        lw = _layer_weights(l, p)

        pr = _inproj(xp, lw, rope_p)
        r3 = lambda a, b_=nb, s_=seq: a.reshape(b_, s_, a.shape[-1])
        br = {"gt": pr["gt"]}
        br["oa"] = _band_prompt(r3(pr["aq"]), r3(pr["ak"]), r3(pr["av"]), lw["a_rel_bias"]).reshape(nb * seq, A_W)
        br["ob"] = _diff_prompt(r3(pr["bq"]), r3(pr["bkh"]), r3(pr["bvh"]), lw["b_lambda"], lw["b_subln"],
                                lam_init).reshape(nb * seq, B_V_W)
        oc, c_fin = _gla(r3(pr["cq"]), r3(pr["ck"]), r3(pr["cv"]), r3(pr["cg"]), r3(pr["cr"]),
                         jnp.zeros((nb, C_QK_W, C_DV), F32), lw["c_on"], c_tile)
        br["oc"] = oc.reshape(nb * seq, C_V_W)
        mk, mv = _mem_kv(mem_prompt, lw)
        br["om"] = _dense_attention(r3(pr["mq"]), mk, mv, None, min(M_Q_TILE, seq)).reshape(nb * seq, M_W)
        outs["a_kp"].append(r3(pr["ak"])[:, seq - a_keep:].reshape(nb, a_keep, A_HEADS, HEAD_DIM))
        outs["a_vp"].append(r3(pr["av"])[:, seq - a_keep:].reshape(nb, a_keep, A_HEADS, HEAD_DIM))
        outs["b_kp"].append(pr["bk"].reshape(nb, seq, B_HEADS, 2, HEAD_DIM))
        outs["b_vp"].append(pr["bv"].reshape(nb, seq, B_HEADS, B_VDIM))
        outs["c_sp"].append(c_fin.reshape(nb, C_HEADS, C_DK, C_DV))
        outs["m_kp"].append(mk.reshape(nb, n_mem, M_HEADS, HEAD_DIM))
        outs["m_vp"].append(mv.reshape(nb, n_mem, M_HEADS, HEAD_DIM))
        xp = _channel_mixer(l, xp, br, lw)

        sr = _inproj(xs, lw, rope_s)
        s3 = lambda a: a.reshape(sb, t_new, a.shape[-1])
        bs = {"gt": sr["gt"]}
        ka = jnp.concatenate([cache_a_k[l].reshape(sb, A_PAST, A_W), s3(sr["ak"])], axis=1)
        va = jnp.concatenate([cache_a_v[l].reshape(sb, A_PAST, A_W), s3(sr["av"])], axis=1)
        dist = jnp.arange(t_new)[:, None] + A_PAST - jnp.arange(A_PAST + t_new)[None, :]
        bias_s = lw["a_rel_bias"][:, jnp.clip(dist, -REL_CLIP, REL_CLIP) + REL_CLIP].astype(F32)
        bs["oa"] = _dense_attention(s3(sr["aq"]), ka, va, bias_s, t_new).reshape(sb * t_new, A_W)
        bs["ob"] = _diff_sample(s3(sr["bq"]), cache_b_k[l].reshape(sb, past, B_QK_W),
                                cache_b_v[l].reshape(sb, past, B_V_W), s3(sr["bkh"]), s3(sr["bvh"]),
                                lw["b_lambda"], lw["b_subln"], lam_init).reshape(sb * t_new, B_V_W)
        oc, c_new = _gla(s3(sr["cq"]), s3(sr["ck"]), s3(sr["cv"]), s3(sr["cg"]), s3(sr["cr"]),
                         state_c[l].reshape(sb, C_QK_W, C_DV), lw["c_on"], t_new)
        bs["oc"] = oc.reshape(sb * t_new, C_V_W)
        bs["om"] = _dense_attention(s3(sr["mq"]), cache_mem_k[l].reshape(sb, n_mem, M_W),
                                    cache_mem_v[l].reshape(sb, n_mem, M_W), None, t_new).reshape(sb * t_new, M_W)
        outs["a_ks"].append(sr["ak"].reshape(sb, t_new, A_HEADS, HEAD_DIM))
        outs["a_vs"].append(sr["av"].reshape(sb, t_new, A_HEADS, HEAD_DIM))
        outs["b_ks"].append(sr["bk"].reshape(sb, t_new, B_HEADS, 2, HEAD_DIM))
        outs["b_vs"].append(sr["bv"].reshape(sb, t_new, B_HEADS, B_VDIM))
        outs["c_ss"].append(c_new.reshape(sb, C_HEADS, C_DK, C_DV))
        xs = _channel_mixer(l, xs, bs, lw)

    st = {k: jnp.stack(v) for k, v in outs.items()}
    return (xp.reshape(nb, seq, d), xs.reshape(sb, t_new, d),
            st["a_kp"], st["a_vp"], st["a_ks"], st["a_vs"], st["b_kp"], st["b_vp"], st["b_ks"], st["b_vs"],
            st["c_sp"], st["c_ss"], st["m_kp"], st["m_vp"])
```

```python
import functools
import math

import numpy as np
import jax
import jax.numpy as jnp
from jax import lax
from jax.experimental import pallas as pl
from jax.experimental.pallas import tpu as pltpu

F32 = jnp.float32
BF16 = jnp.bfloat16
I32 = jnp.int32

CHUNK = 64
HEAD_DIM = 64
A_HEADS = 8
A_BAND_CHUNKS = 8
A_PAST = A_BAND_CHUNKS * CHUNK
REL_CLIP = 128
B_HEADS = 4
B_VDIM = 2 * HEAD_DIM
ROT_DIM = HEAD_DIM // 4
ROPE_THETA = 500000.0
C_HEADS = 4
C_DK = 64
C_DV = 128
C_GATE_RANK = 16
C_TAU = 16.0
M_HEADS = 4
N_BRANCH = 4
N_EXPERTS = 8
TOP_K = 2
EPS = 1e-6
NEG_INF = -1e30

A_W = A_HEADS * HEAD_DIM
B_QK_W = B_HEADS * 2 * HEAD_DIM
B_V_W = B_HEADS * B_VDIM
C_QK_W = C_HEADS * C_DK
C_V_W = C_HEADS * C_DV
M_W = M_HEADS * HEAD_DIM

LANES = 128
VMEM_LIMIT_BYTES = 56 * 1024 * 1024

TOKEN_TILE = 256
A_Q_TILE = 256
B_Q_TILE = 1024
B_K_TILE = 1024
B_CACHE_TILE = 1024
C_TILE = 256
M_Q_TILE = 512
FF_CHUNK = 256
MOE_ROWS = 256
MOE_FF_CHUNK = 512
QK_SCALE = HEAD_DIM ** -0.5
LOG2_E = math.log2(math.e)

_SEG = {}
_off = 0
for _name, _w in (("aq", A_W), ("ak", A_W), ("av", A_W), ("bq", B_QK_W), ("bk", B_QK_W), ("bv", B_V_W),
                  ("cq", C_QK_W), ("ck", C_QK_W), ("cv", C_V_W), ("cr", C_V_W), ("mq", M_W)):
    _SEG[_name] = (_off, _w)
    _off += _w
GATE_OFF = _off


def _cparams(*sem):
    return pltpu.CompilerParams(dimension_semantics=sem, vmem_limit_bytes=VMEM_LIMIT_BYTES)


def _const_spec(shape):
    nd = len(shape)
    return pl.BlockSpec(shape, lambda *_: (0,) * nd, pipeline_mode=pl.Buffered(1))


def _dot(a, b):
    return jnp.dot(a, b, preferred_element_type=F32)


def _dot_nt(a, b):
    return lax.dot_general(a, b, (((1,), (1,)), ((), ())), preferred_element_type=F32)


def _dot_tn(a, b):
    return lax.dot_general(a, b, (((0,), (0,)), ((), ())), preferred_element_type=F32)


def _rms(x, gain):
    return x * lax.rsqrt(jnp.mean(x * x, axis=-1, keepdims=True) + EPS) * gain


def _head_norm(y, bd_ref, gain):
    w = y.shape[-1]
    ms = _dot((y * y).astype(BF16), bd_ref[:w, :w])
    return y * lax.rsqrt(ms + EPS) * gain


def _sigmoid(x):
    return 1.0 / (1.0 + jnp.exp(-x))


def _inproj_body(x_ref, g1_ref, w_ref, bd_ref, aqn_ref, akn_ref, bqn_ref, bkn_ref, mqn_ref, cwa_ref, cba_ref,
                 rc_ref, rs1_ref, rs2_ref,
                 aq_o, ak_o, av_o, bq_o, bk_o, bkh_o, bv_o, bvh_o, cq_o, ck_o, cv_o, cg_o, cr_o, mq_o, gt_o):
    x = x_ref[...]
    h = _rms(x, g1_ref[...]).astype(BF16)
    d_model = x.shape[-1]

    def proj(name):
        lo, w = _SEG[name]
        return _dot(h, w_ref[:, lo:lo + w])

    def rope(y):
        reps = y.shape[-1] // LANES
        c = jnp.concatenate([rc_ref[...]] * reps, axis=1)
        s1 = jnp.concatenate([rs1_ref[...]] * reps, axis=1)
        s2 = jnp.concatenate([rs2_ref[...]] * reps, axis=1)
        half = ROT_DIM // 2
        return y * c + pltpu.roll(y, y.shape[-1] - half, 1) * s1 + pltpu.roll(y, half, 1) * s2

    aq_o[...] = (_head_norm(proj("aq"), bd_ref, aqn_ref[...]) * QK_SCALE).astype(BF16)
    ak_o[...] = _head_norm(proj("ak"), bd_ref, akn_ref[...])
    av_o[...] = proj("av")
    bq_o[...] = (rope(_head_norm(proj("bq"), bd_ref, bqn_ref[...])) * (QK_SCALE * LOG2_E)).astype(BF16)
    bk = rope(_head_norm(proj("bk"), bd_ref, bkn_ref[...]))
    bk_o[...] = bk
    bkh_o[...] = bk.astype(BF16)
    bv = proj("bv")
    bv_o[...] = bv
    bvh_o[...] = bv.astype(BF16)
    cq_o[...] = proj("cq") * (C_DK ** -0.5)
    ck_o[...] = proj("ck")
    cv_o[...] = proj("cv").astype(BF16)
    cr_o[...] = proj("cr").astype(BF16)
    mq_o[...] = (_head_norm(proj("mq"), bd_ref, mqn_ref[...]) * QK_SCALE).astype(BF16)
    ca = _dot(h, w_ref[:, GATE_OFF + N_BRANCH * d_model:])
    z = _dot(ca.astype(BF16), cwa_ref[...]) + cba_ref[...]
    cg_o[...] = (jnp.minimum(z, 0.0) - jnp.log1p(jnp.exp(-jnp.abs(z)))) * (1.0 / C_TAU)
    for c in range(N_BRANCH):
        lo = GATE_OFF + c * d_model
        gt_o[:, c * d_model:(c + 1) * d_model] = _sigmoid(_dot(h, w_ref[:, lo:lo + d_model])).astype(BF16)


def _inproj(x, lw, rope_tabs):
    t, d = x.shape
    tm = TOKEN_TILE
    wcols = lw["w_in"].shape[1]
    row = lambda w: pl.BlockSpec((tm, w), lambda i: (i, 0))
    outs = [("aq", A_W, BF16), ("ak", A_W, F32), ("av", A_W, F32), ("bq", B_QK_W, BF16), ("bk", B_QK_W, F32),
            ("bkh", B_QK_W, BF16), ("bv", B_V_W, F32), ("bvh", B_V_W, BF16), ("cq", C_QK_W, F32),
            ("ck", C_QK_W, F32), ("cv", C_V_W, BF16), ("cg", C_QK_W, F32), ("cr", C_V_W, BF16),
            ("mq", M_W, BF16), ("gt", N_BRANCH * d, BF16)]
    res = pl.pallas_call(
        _inproj_body,
        grid=(t // tm,),
        in_specs=[row(d), _const_spec((1, d)), _const_spec((d, wcols)), _const_spec((A_W, A_W)),
                  _const_spec((1, A_W)), _const_spec((1, A_W)), _const_spec((1, B_QK_W)), _const_spec((1, B_QK_W)),
                  _const_spec((1, M_W)), _const_spec((LANES, C_QK_W)), _const_spec((1, C_QK_W)),
                  row(LANES), row(LANES), row(LANES)],
        out_specs=[row(w) for _, w, _ in outs],
        out_shape=[jax.ShapeDtypeStruct((t, w), dt) for _, w, dt in outs],
        compiler_params=_cparams("parallel"),
        name="inproj",
    )(x, lw["norm1_g"], lw["w_in"], lw["bd"], lw["a_qn"], lw["a_kn"], lw["b_qn"], lw["b_kn"], lw["m_qn"],
      lw["c_wa"], lw["c_ba"], *rope_tabs)
    return {n: r for (n, _, _), r in zip(outs, res)}


def _pair_attention(q2, k2, v2, bias_even, bias_odd):
    tq = q2.shape[0]
    lane = lax.broadcasted_iota(I32, q2.shape, 1)
    zero = jnp.zeros_like(q2)
    qq = jnp.concatenate([jnp.where(lane < HEAD_DIM, q2, zero), jnp.where(lane >= HEAD_DIM, q2, zero)], axis=0)
    s = _dot_nt(qq, k2)
    if bias_even is not None:
        s = s + jnp.concatenate([bias_even, bias_odd], axis=0)
    m = jnp.max(s, axis=-1, keepdims=True)
    p = jnp.exp(s - m)
    l = jnp.sum(p, axis=-1, keepdims=True)
    o = _dot(p.astype(BF16), v2) / l
    return jnp.where(lane < HEAD_DIM, o[:tq], o[tq:])


def _band_body(q_ref, k0_ref, k1_ref, k2_ref, v0_ref, v1_ref, v2_ref, bias_ref, o_ref):
    i = pl.program_id(1)
    tq = q_ref.shape[1]
    k = jnp.concatenate([k0_ref[0], k1_ref[0], k2_ref[0]], axis=0).astype(BF16)
    v = jnp.concatenate([v0_ref[0], v1_ref[0], v2_ref[0]], axis=0).astype(BF16)
    col = lax.broadcasted_iota(I32, (tq, 3 * tq), 1)
    pad = jnp.where(col >= 2 * tq - i * tq, 0.0, NEG_INF)
    for p in range(A_HEADS // 2):
        sl = slice(p * LANES, (p + 1) * LANES)
        o_ref[0, :, sl] = _pair_attention(q_ref[0, :, sl], k[:, sl], v[:, sl],
                                          bias_ref[2 * p] + pad, bias_ref[2 * p + 1] + pad).astype(o_ref.dtype)


def _rel_bias(table, n_q, n_k, offset):
    period = n_q + n_k
    k = np.arange(period)
    dist = np.where(k < n_k, offset - k, offset - k + period)
    w = table[:, np.clip(dist, -REL_CLIP, REL_CLIP) + REL_CLIP].astype(F32)
    heads = table.shape[0]
    return jnp.tile(w, (1, n_q))[:, :n_q * (period - 1)].reshape(heads, n_q, period - 1)[:, :, :n_k]


def _band_bias(table, tq):
    r = np.arange(tq)[:, None]
    c = np.arange(3 * tq)[None, :]
    qc = (r + 2 * tq) // CHUNK
    kc = c // CHUNK
    vis = (kc <= qc) & (kc >= qc - A_BAND_CHUNKS)
    return jnp.where(jnp.asarray(vis)[None], _rel_bias(table, tq, 3 * tq, 2 * tq), NEG_INF)


def _band_prompt(aq, ak, av, table):
    b, s, w = aq.shape
    tq = A_Q_TILE
    assert A_PAST == 2 * tq and s % tq == 0
    qspec = pl.BlockSpec((1, tq, w), lambda bi, i: (bi, i, 0))
    kspec = lambda back: pl.BlockSpec((1, tq, w), lambda bi, i: (bi, jnp.maximum(i - back, 0), 0))
    return pl.pallas_call(
        _band_body,
        grid=(b, s // tq),
        in_specs=[qspec, kspec(2), kspec(1), kspec(0), kspec(2), kspec(1), kspec(0),
                  _const_spec((A_HEADS, tq, 3 * tq))],
        out_specs=qspec,
        out_shape=jax.ShapeDtypeStruct((b, s, w), BF16),
        compiler_params=_cparams("parallel", "parallel"),
        name="band_attention",
    )(aq, ak, ak, ak, av, av, av, _band_bias(table, tq))


def _dense_attn_body(*refs, heads, has_bias):
    if has_bias:
        q_ref, k_ref, v_ref, bias_ref, o_ref = refs
    else:
        q_ref, k_ref, v_ref, o_ref = refs
    k = k_ref[0].astype(BF16)
    v = v_ref[0].astype(BF16)
    for p in range(heads // 2):
        sl = slice(p * LANES, (p + 1) * LANES)
        be, bo = (bias_ref[2 * p], bias_ref[2 * p + 1]) if has_bias else (None, None)
        o_ref[0, :, sl] = _pair_attention(q_ref[0, :, sl], k[:, sl], v[:, sl], be, bo).astype(o_ref.dtype)


def _dense_attention(q, k, v, bias, tq):
    b, sq, w = q.shape
    sk = k.shape[1]
    heads = w // HEAD_DIM
    qspec = pl.BlockSpec((1, tq, w), lambda bi, i: (bi, i, 0))
    kspec = pl.BlockSpec((1, sk, w), lambda bi, i: (bi, 0, 0))
    in_specs = [qspec, kspec, kspec]
    args = [q, k, v]
    if bias is not None:
        in_specs.append(_const_spec(bias.shape))
        args.append(bias)
    return pl.pallas_call(
        functools.partial(_dense_attn_body, heads=heads, has_bias=bias is not None),
        grid=(b, sq // tq),
        in_specs=in_specs,
        out_specs=qspec,
        out_shape=jax.ShapeDtypeStruct((b, sq, w), BF16),
        compiler_params=_cparams("parallel", "parallel"),
        name="dense_attention",
    )(*args)


def _memkv_body(mem_ref, g_ref, w_ref, bd_ref, kn_ref, k_o, v_o):
    h = _rms(mem_ref[0], g_ref[...]).astype(BF16)
    kv = _dot(h, w_ref[...])
    k_o[0] = _head_norm(kv[:, :M_W], bd_ref, kn_ref[...])
    v_o[0] = kv[:, M_W:]


def _mem_kv(mem, lw):
    b, n, d = mem.shape
    spec = pl.BlockSpec((1, n, M_W), lambda bi: (bi, 0, 0))
    return pl.pallas_call(
        _memkv_body,
        grid=(b,),
        in_specs=[pl.BlockSpec((1, n, d), lambda bi: (bi, 0, 0)), _const_spec((1, d)), _const_spec((d, 2 * M_W)),
                  _const_spec((A_W, A_W)), _const_spec((1, M_W))],
        out_specs=[spec, spec],
        out_shape=[jax.ShapeDtypeStruct((b, n, M_W), F32)] * 2,
        compiler_params=_cparams("parallel"),
        name="mem_kv",
    )(mem, lw["mem_norm"], lw["w_mem_kv"], lw["bd"], lw["m_kn"])


def _diff_init(q_ref, qq_scr, m_scr, l_scr, acc_scr):
    q = q_ref[0]
    lane = lax.broadcasted_iota(I32, q.shape, 1)
    zero = jnp.zeros_like(q)
    tq = q.shape[0]
    qq_scr[:tq] = jnp.where(lane < HEAD_DIM, q, zero)
    qq_scr[tq:] = jnp.where(lane >= HEAD_DIM, q, zero)
    m_scr[...] = jnp.full(m_scr.shape, NEG_INF, F32)
    l_scr[...] = jnp.zeros(l_scr.shape, F32)
    acc_scr[...] = jnp.zeros(acc_scr.shape, F32)


def _diff_step(k, v, qq_scr, m_scr, l_scr, acc_scr, visible):
    s = _dot_nt(qq_scr[...], k)
    if visible is not None:
        s = jnp.where(visible, s, NEG_INF)
    m_prev = m_scr[...]
    m_new = jnp.maximum(m_prev, jnp.max(s, axis=-1, keepdims=True))
    alpha = jnp.exp2(m_prev - m_new)
    tk = s.shape[-1]
    m_wide = m_new[:, :tk] if tk <= LANES else jnp.concatenate([m_new] * (tk // LANES), axis=1)
    p = jnp.exp2(s - m_wide)
    l_scr[...] = alpha * l_scr[...] + jnp.sum(p, axis=-1, keepdims=True)
    acc_scr[...] = alpha * acc_scr[...] + _dot(p.astype(BF16), v)
    m_scr[...] = m_new


def _diff_finish(lam_ref, sub_ref, o_ref, l_scr, acc_scr, lam_init):
    tq = o_ref.shape[1]
    lam_p = lam_ref[...]
    lam = (jnp.exp(jnp.sum(lam_p[0:1] * lam_p[1:2], axis=-1, keepdims=True))
           - jnp.exp(jnp.sum(lam_p[2:3] * lam_p[3:4], axis=-1, keepdims=True)) + lam_init)
    o = acc_scr[:tq] / l_scr[:tq] - lam * (acc_scr[tq:] / l_scr[tq:])
    o_ref[0] = (_rms(o, sub_ref[...]) * (1.0 - lam_init)).astype(o_ref.dtype)


def _diff_prompt_body(qi_ref, ki_ref, q_ref, k_ref, v_ref, lam_ref, sub_ref, o_ref,
                      qq_scr, m_scr, l_scr, acc_scr, *, lam_init):
    s_id = pl.program_id(2)
    qi = qi_ref[s_id]
    ki = ki_ref[s_id]
    tq = q_ref.shape[1]
    tk = k_ref.shape[1]

    @pl.when(ki == 0)
    def _():
        _diff_init(q_ref, qq_scr, m_scr, l_scr, acc_scr)

    needs_mask = (ki + 1) * tk > qi * tq

    @pl.when(needs_mask)
    def _():
        row = lax.broadcasted_iota(I32, (2 * tq, tk), 0)
        row = jnp.where(row >= tq, row - tq, row)
        col = lax.broadcasted_iota(I32, (2 * tq, tk), 1)
        chunk_shift = int(math.log2(CHUNK))
        visible = ((ki * tk + col) >> chunk_shift) <= ((qi * tq + row) >> chunk_shift)
        _diff_step(k_ref[0], v_ref[0], qq_scr, m_scr, l_scr, acc_scr, visible)

    @pl.when(jnp.logical_not(needs_mask))
    def _():
        _diff_step(k_ref[0], v_ref[0], qq_scr, m_scr, l_scr, acc_scr, None)

    last_ki = ((qi + 1) * tq - 1) // tk

    @pl.when(ki == last_ki)
    def _():
        _diff_finish(lam_ref, sub_ref, o_ref, l_scr, acc_scr, lam_init)


def _diff_prompt(bq, bkh, bvh, lam_p, sub, lam_init):
    b, s, w = bq.shape
    tq = min(B_Q_TILE, s)
    tk = min(B_K_TILE, s)
    steps = [(qi, ki) for qi in range(s // tq) for ki in range(((qi + 1) * tq - 1) // tk + 1)]
    qi_tab = jnp.asarray([p[0] for p in steps], I32)
    ki_tab = jnp.asarray([p[1] for p in steps], I32)
    qspec = pl.BlockSpec((1, tq, LANES), lambda bi, h, st, qt, kt: (bi, qt[st], h))
    kspec = pl.BlockSpec((1, tk, LANES), lambda bi, h, st, qt, kt: (bi, kt[st], h))
    cspec = lambda shp: pl.BlockSpec(shp, lambda bi, h, st, qt, kt: (0, 0))
    return pl.pallas_call(
        functools.partial(_diff_prompt_body, lam_init=lam_init),
        grid_spec=pltpu.PrefetchScalarGridSpec(
            num_scalar_prefetch=2,
            grid=(b, B_HEADS, len(steps)),
            in_specs=[qspec, kspec, kspec, cspec((4, HEAD_DIM)), cspec((1, B_VDIM))],
            out_specs=qspec,
            scratch_shapes=[pltpu.VMEM((2 * tq, LANES), BF16), pltpu.VMEM((2 * tq, LANES), F32),
                            pltpu.VMEM((2 * tq, LANES), F32), pltpu.VMEM((2 * tq, B_VDIM), F32)]),
        out_shape=jax.ShapeDtypeStruct((b, s, w), BF16),
        compiler_params=_cparams("parallel", "parallel", "arbitrary"),
        name="diff_attention",
    )(qi_tab, ki_tab, bq, bkh, bvh, lam_p, sub)


def _diff_sample_body(q_ref, ck_ref, cv_ref, nk_ref, nv_ref, lam_ref, sub_ref, o_ref,
                      qq_scr, m_scr, l_scr, acc_scr, *, lam_init):
    t = pl.program_id(2)

    @pl.when(t == 0)
    def _():
        _diff_init(q_ref, qq_scr, m_scr, l_scr, acc_scr)

    _diff_step(ck_ref[0].astype(BF16), cv_ref[0].astype(BF16), qq_scr, m_scr, l_scr, acc_scr, None)

    @pl.when(t == pl.num_programs(2) - 1)
    def _():
        _diff_step(nk_ref[0], nv_ref[0], qq_scr, m_scr, l_scr, acc_scr, None)
        _diff_finish(lam_ref, sub_ref, o_ref, l_scr, acc_scr, lam_init)


def _diff_sample(bq, cache_k, cache_v, layer, bkh, bvh, lam_p, sub, lam_init):
    b, tq, w = bq.shape
    past = cache_k.shape[2]
    tc = min(B_CACHE_TILE, past)
    assert past % tc == 0 and tq == CHUNK and past % CHUNK == 0
    qspec = pl.BlockSpec((1, tq, LANES), lambda bi, h, t: (bi, 0, h))
    cspec = pl.BlockSpec((None, 1, tc, LANES), lambda bi, h, t: (layer, bi, t, h))
    pspec = lambda shp: pl.BlockSpec(shp, lambda bi, h, t: (0, 0))
    return pl.pallas_call(
        functools.partial(_diff_sample_body, lam_init=lam_init),
        grid=(b, B_HEADS, past // tc),
        in_specs=[qspec, cspec, cspec, qspec, qspec, pspec((4, HEAD_DIM)), pspec((1, B_VDIM))],
        out_specs=qspec,
        scratch_shapes=[pltpu.VMEM((2 * tq, LANES), BF16), pltpu.VMEM((2 * tq, LANES), F32),
                        pltpu.VMEM((2 * tq, LANES), F32), pltpu.VMEM((2 * tq, B_VDIM), F32)],
        out_shape=jax.ShapeDtypeStruct((b, tq, w), BF16),
        compiler_params=_cparams("parallel", "parallel", "arbitrary"),
        name="diff_attention_sample",
    )(bq, cache_k, cache_v, bkh, bvh, lam_p, sub)


def _gla_decay_matrix(t):
    levels = int(math.log2(t))
    d = np.zeros((levels + 2, t, t), np.float32)
    u = np.arange(t)[None, :]
    i = np.arange(t)[:, None]
    for lv in range(levels):
        g = t >> lv
        base = i - i % g
        r = base + g // 2 - 1
        second = (i % g) >= g // 2
        d[lv] = np.where(second, (u > r) & (u <= i), (u > i) & (u <= r))
    d[levels] = u <= i
    d[levels + 1] = u > i
    return jnp.asarray(d.reshape((levels + 2) * t, t), BF16)


def _gla_body(q_ref, k_ref, v_ref, g_ref, r_ref, s0_ref, d_ref, on_ref, o_ref, sf_ref, s_scr):
    c = pl.program_id(1)
    t = q_ref.shape[1]
    levels = int(math.log2(t))

    @pl.when(c == 0)
    def _():
        s_scr[...] = s0_ref[0]

    q = q_ref[0]
    k = k_ref[0]
    g = g_ref[0]
    g_hi = g.astype(BF16)
    g_lo = (g - g_hi.astype(F32)).astype(BF16)
    g2 = jnp.concatenate([g_hi, g_lo], axis=1)

    def exponent(block):
        e = _dot(d_ref[block * t:(block + 1) * t, :], g2)
        return e[:, :C_QK_W] + e[:, C_QK_W:]

    lane = lax.broadcasted_iota(I32, (t, C_QK_W), 1)
    head_lanes = [(lane >= h * C_DK) & (lane < (h + 1) * C_DK) for h in range(C_HEADS)]
    tok = lax.broadcasted_iota(I32, (t, C_QK_W), 0)
    ri = lax.broadcasted_iota(I32, (t, t), 0)
    ci = lax.broadcasted_iota(I32, (t, t), 1)
    zero = jnp.zeros((t, C_QK_W), BF16)
    kb = k.astype(BF16)
    qb = q.astype(BF16)

    att = [jnp.where(ri == ci, _dot_nt(jnp.where(head_lanes[h], qb, zero), kb), 0.0) for h in range(C_HEADS)]
    for lv in range(levels):
        shift = levels - lv
        w = jnp.exp(exponent(lv))
        second = ((tok >> (shift - 1)) & 1) == 1
        ql = jnp.where(second, q * w, 0.0).astype(BF16)
        kl = jnp.where(second, 0.0, k * w).astype(BF16)
        same = (ri >> shift) == (ci >> shift)
        for h in range(C_HEADS):
            att[h] = att[h] + jnp.where(same, _dot_nt(jnp.where(head_lanes[h], ql, zero), kl), 0.0)

    state = s_scr[...]
    state_b = state.astype(BF16)
    q_in = (q * jnp.exp(exponent(levels))).astype(BF16)
    k_out = (k * jnp.exp(exponent(levels + 1))).astype(BF16)
    ones = jnp.ones((t, C_DV), BF16)
    decay = jnp.exp(_dot_tn(g_hi, ones) + _dot_tn(g_lo, ones))
    srow = lax.broadcasted_iota(I32, (C_QK_W, C_DV), 0)
    new_state = decay * state
    for h in range(C_HEADS):
        sl = slice(h * C_DV, (h + 1) * C_DV)
        v = v_ref[0, :, sl]
        o = _dot(att[h].astype(BF16), v) + _dot(jnp.where(head_lanes[h], q_in, zero), state_b)
        r = r_ref[0, :, sl].astype(F32)
        o_ref[0, :, sl] = (_rms(o, on_ref[...]) * (r * _sigmoid(r))).astype(o_ref.dtype)
        upd = _dot_tn(k_out, v)
        new_state = new_state + jnp.where((srow >= h * C_DK) & (srow < (h + 1) * C_DK), upd, 0.0)
    s_scr[...] = new_state

    @pl.when(c == pl.num_programs(1) - 1)
    def _():
        sf_ref[0] = new_state


def _gla(cq, ck, cv, cg, cr, state0, c_on, t):
    b, s, _ = cq.shape
    levels = int(math.log2(t))
    assert 1 << levels == t and s % t == 0
    tok = lambda w: pl.BlockSpec((1, t, w), lambda bi, c: (bi, c, 0))
    sspec = pl.BlockSpec((1, C_QK_W, C_DV), lambda bi, c: (bi, 0, 0))
    return pl.pallas_call(
        _gla_body,
        grid=(b, s // t),
        in_specs=[tok(C_QK_W), tok(C_QK_W), tok(C_V_W), tok(C_QK_W), tok(C_V_W), sspec,
                  _const_spec(((levels + 2) * t, t)), _const_spec((1, C_DV))],
        out_specs=[tok(C_V_W), sspec],
        out_shape=[jax.ShapeDtypeStruct((b, s, C_V_W), BF16), jax.ShapeDtypeStruct((b, C_QK_W, C_DV), F32)],
        scratch_shapes=[pltpu.VMEM((C_QK_W, C_DV), F32)],
        compiler_params=_cparams("parallel", "arbitrary"),
        name="gla",
    )(cq, ck, cv, cg, cr, state0, _gla_decay_matrix(t), c_on)


def _merge_residual(x_ref, oa_ref, ob_ref, oc_ref, om_ref, gt_ref, wa_ref, wb_ref, wc_ref, wm_ref, wo_ref):
    d = x_ref.shape[-1]
    gate = lambda c: gt_ref[:, c * d:(c + 1) * d].astype(F32)
    y = (gate(0) * _dot(oa_ref[...], wa_ref[...]) + gate(1) * _dot(ob_ref[...], wb_ref[...])
         + gate(2) * _dot(oc_ref[...], wc_ref[...]) + gate(3) * _dot(om_ref[...], wm_ref[...]))
    return x_ref[...] + _dot(y.astype(BF16), wo_ref[...])


def _merge_ffn_body(x_ref, oa_ref, ob_ref, oc_ref, om_ref, gt_ref, wa_ref, wb_ref, wc_ref, wm_ref, wo_ref,
                    g2_ref, w13_ref, w2_ref, o_ref):
    x1 = _merge_residual(x_ref, oa_ref, ob_ref, oc_ref, om_ref, gt_ref, wa_ref, wb_ref, wc_ref, wm_ref, wo_ref)
    h = _rms(x1, g2_ref[...]).astype(BF16)
    d_ff = w2_ref.shape[0]
    acc = x1
    for c in range(d_ff // FF_CHUNK):
        lo = c * FF_CHUNK
        a = _dot(h, w13_ref[:, lo:lo + FF_CHUNK])
        b = _dot(h, w13_ref[:, d_ff + lo:d_ff + lo + FF_CHUNK])
        acc = acc + _dot((a * _sigmoid(a) * b).astype(BF16), w2_ref[lo:lo + FF_CHUNK, :])
    o_ref[...] = acc


def _merge_router_body(x_ref, oa_ref, ob_ref, oc_ref, om_ref, gt_ref, wa_ref, wb_ref, wc_ref, wm_ref, wo_ref,
                       g2_ref, rh_ref, rl_ref, x1_o, h_o, idx_o, wt_o):
    x1 = _merge_residual(x_ref, oa_ref, ob_ref, oc_ref, om_ref, gt_ref, wa_ref, wb_ref, wc_ref, wm_ref, wo_ref)
    x1_o[...] = x1
    h = _rms(x1, g2_ref[...])
    h_o[...] = h
    h_hi = h.astype(BF16)
    h_lo = (h - h_hi.astype(F32)).astype(BF16)
    logits = _dot(h_hi, rh_ref[...]) + _dot(h_lo, rh_ref[...]) + _dot(h_hi, rl_ref[...])
    lane = lax.broadcasted_iota(I32, logits.shape, 1)
    lane_f = lane.astype(F32)
    neg = jnp.float32(-jnp.inf)
    lg = jnp.where(lane < N_EXPERTS, logits, neg)
    m1 = jnp.max(lg, axis=-1, keepdims=True)
    i1 = jnp.min(jnp.where(lg == m1, lane_f, float(LANES)), axis=-1, keepdims=True)
    lg2 = jnp.where(lane_f == i1, neg, lg)
    m2 = jnp.max(lg2, axis=-1, keepdims=True)
    i2 = jnp.min(jnp.where(lg2 == m2, lane_f, float(LANES)), axis=-1, keepdims=True)
    e = jnp.exp(m2 - m1)
    w1 = 1.0 / (1.0 + e)
    idx_o[...] = jnp.where(lane == 0, i1, jnp.where(lane == 1, i2, 0.0)).astype(I32)
    wt_o[...] = jnp.where(lane == 0, w1, jnp.where(lane == 1, e * w1, 0.0))


def _merge_specs(t, d):
    tm = TOKEN_TILE
    row = lambda w: pl.BlockSpec((tm, w), lambda i: (i, 0))
    specs = [row(d), row(A_W), row(B_V_W), row(C_V_W), row(M_W), row(N_BRANCH * d),
             _const_spec((A_W, d)), _const_spec((B_V_W, d)), _const_spec((C_V_W, d)), _const_spec((M_W, d)),
             _const_spec((d, d)), _const_spec((1, d))]
    return tm, row, specs


def _merge_args(x, br, lw):
    return (x, br["oa"], br["ob"], br["oc"], br["om"], br["gt"], lw["w_a"], lw["w_b"], lw["w_c"], lw["w_m"],
            lw["w_o"], lw["norm2_g"])


def _merge_ffn(x, br, lw):
    t, d = x.shape
    tm, row, specs = _merge_specs(t, d)
    d_ff = lw["ffn_w2"].shape[0]
    assert d_ff % FF_CHUNK == 0
    return pl.pallas_call(
        _merge_ffn_body,
        grid=(t // tm,),
        in_specs=specs + [_const_spec((d, 2 * d_ff)), _const_spec((d_ff, d))],
        out_specs=row(d),
        out_shape=jax.ShapeDtypeStruct((t, d), F32),
        compiler_params=_cparams("parallel"),
        name="merge_ffn",
    )(*_merge_args(x, br, lw), lw["ffn_w13"], lw["ffn_w2"])


def _merge_router(x, br, lw):
    t, d = x.shape
    tm, row, specs = _merge_specs(t, d)
    return pl.pallas_call(
        _merge_router_body,
        grid=(t // tm,),
        in_specs=specs + [_const_spec((d, LANES)), _const_spec((d, LANES))],
        out_specs=[row(d), row(d), row(LANES), row(LANES)],
        out_shape=[jax.ShapeDtypeStruct((t, d), F32), jax.ShapeDtypeStruct((t, d), F32),
                   jax.ShapeDtypeStruct((t, LANES), I32), jax.ShapeDtypeStruct((t, LANES), F32)],
        compiler_params=_cparams("parallel"),
        name="merge_router",
    )(*_merge_args(x, br, lw), lw["router_hi"], lw["router_lo"])


def _row_gather_copy(src_hbm, idx_ref, n, dst, sem):
    def issue(r, carry):
        pltpu.make_async_copy(src_hbm.at[pl.ds(idx_ref[0, 0, r], 1)], dst.at[pl.ds(r, 1)], sem).start()
        return carry
    lax.fori_loop(0, n, issue, 0, unroll=8)


def _row_gather_wait(dst, sem):
    pltpu.make_async_copy(dst, dst, sem).wait()


def _moe_body(blk_e_ref, cur_ref, nxt_ref, h_hbm, w13_ref, w2_ref, y_ref, xbuf, sem):
    i = pl.program_id(0)
    n = pl.num_programs(0)
    rows = xbuf.shape[1]
    slot = i % 2

    @pl.when(i == 0)
    def _():
        _row_gather_copy(h_hbm, cur_ref, rows, xbuf.at[0], sem.at[0])

    @pl.when(i + 1 < n)
    def _():
        _row_gather_copy(h_hbm, nxt_ref, rows, xbuf.at[1 - slot], sem.at[1 - slot])

    _row_gather_wait(xbuf.at[slot], sem.at[slot])
    x = xbuf[slot].astype(BF16)
    d_ff = w2_ref.shape[1]
    acc = jnp.zeros(y_ref.shape, F32)
    for c in range(d_ff // MOE_FF_CHUNK):
        lo = c * MOE_FF_CHUNK
        a = _dot(x, w13_ref[0, :, lo:lo + MOE_FF_CHUNK])
        b = _dot(x, w13_ref[0, :, d_ff + lo:d_ff + lo + MOE_FF_CHUNK])
        acc = acc + _dot((a * _sigmoid(a) * b).astype(BF16), w2_ref[0, lo:lo + MOE_FF_CHUNK, :])
    y_ref[...] = acc


def _moe_experts(h, row_tok, blk_e, w13, w2, layer):
    t, d = h.shape
    n_blocks = blk_e.shape[0]
    rows = MOE_ROWS
    d_ff = w2.shape[2]
    assert d_ff % MOE_FF_CHUNK == 0
    tok3 = row_tok.reshape(n_blocks, 1, rows)
    smem = lambda f: pl.BlockSpec((1, 1, rows), f, memory_space=pltpu.SMEM)
    return pl.pallas_call(
        _moe_body,
        grid_spec=pltpu.PrefetchScalarGridSpec(
            num_scalar_prefetch=1,
            grid=(n_blocks,),
            in_specs=[smem(lambda i, be: (i, 0, 0)),
                      smem(lambda i, be: (jnp.minimum(i + 1, n_blocks - 1), 0, 0)),
                      pl.BlockSpec(memory_space=pl.ANY),
                      pl.BlockSpec((None, 1, d, 2 * d_ff), lambda i, be: (layer, be[i], 0, 0)),
                      pl.BlockSpec((None, 1, d_ff, d), lambda i, be: (layer, be[i], 0, 0))],
            out_specs=pl.BlockSpec((rows, d), lambda i, be: (i, 0)),
            scratch_shapes=[pltpu.VMEM((2, rows, d), F32), pltpu.SemaphoreType.DMA((2,))]),
        out_shape=jax.ShapeDtypeStruct((n_blocks * rows, d), F32),
        compiler_params=_cparams("arbitrary"),
        name="moe_experts",
    )(blk_e, tok3, tok3, h, w13, w2)


def _combine_body(cur_ref, nxt_ref, x_ref, wt_ref, y_hbm, o_ref, ybuf, sem):
    i = pl.program_id(0)
    n = pl.num_programs(0)
    tm = x_ref.shape[0]
    slot = i % 2

    @pl.when(i == 0)
    def _():
        _row_gather_copy(y_hbm, cur_ref, 2 * tm, ybuf.at[0], sem.at[0])

    @pl.when(i + 1 < n)
    def _():
        _row_gather_copy(y_hbm, nxt_ref, 2 * tm, ybuf.at[1 - slot], sem.at[1 - slot])

    _row_gather_wait(ybuf.at[slot], sem.at[slot])
    wt = wt_ref[...]
    o_ref[...] = x_ref[...] + wt[:, 0:1] * ybuf[slot, :tm] + wt[:, 1:2] * ybuf[slot, tm:]


def _moe_combine(x1, wt, y, dest):
    t, d = x1.shape
    tm = TOKEN_TILE
    n = t // tm
    dest3 = dest.reshape(n, tm, 2).transpose(0, 2, 1).reshape(n, 1, 2 * tm)
    smem = lambda f: pl.BlockSpec((1, 1, 2 * tm), f, memory_space=pltpu.SMEM)
    row = lambda w: pl.BlockSpec((tm, w), lambda i: (i, 0))
    return pl.pallas_call(
        _combine_body,
        grid=(n,),
        in_specs=[smem(lambda i: (i, 0, 0)), smem(lambda i: (jnp.minimum(i + 1, n - 1), 0, 0)),
                  row(d), row(LANES), pl.BlockSpec(memory_space=pl.ANY)],
        out_specs=row(d),
        out_shape=jax.ShapeDtypeStruct((t, d), F32),
        scratch_shapes=[pltpu.VMEM((2, 2 * tm, d), F32), pltpu.SemaphoreType.DMA((2,))],
        compiler_params=_cparams("arbitrary"),
        name="moe_combine",
    )(dest3, dest3, x1, wt, y)


def _moe(x1, h, idx, wt, w13, w2, layer):
    t, d = h.shape
    top_i = idx[:, :TOP_K]
    flat_e = top_i.reshape(-1)
    onehot = (flat_e[:, None] == jnp.arange(N_EXPERTS, dtype=I32)[None, :]).astype(I32)
    csum = jnp.cumsum(onehot, axis=0)
    rank = jnp.take_along_axis(csum, flat_e[:, None], axis=1)[:, 0] - 1
    counts = csum[-1]
    padded = (counts + MOE_ROWS - 1) // MOE_ROWS * MOE_ROWS
    pad_end = jnp.cumsum(padded)
    pad_start = pad_end - padded
    dest = pad_start[flat_e] + rank
    n_blocks = -(-(t * TOP_K + N_EXPERTS * (MOE_ROWS - 1)) // MOE_ROWS)
    n_rows = n_blocks * MOE_ROWS
    flat_tok = jnp.arange(t * TOP_K, dtype=I32) // TOP_K
    row_tok = jnp.zeros((n_rows,), I32).at[dest].set(flat_tok)
    blk_e = jnp.minimum(jnp.searchsorted(pad_end, jnp.arange(n_blocks, dtype=I32) * MOE_ROWS, side="right"),
                        N_EXPERTS - 1).astype(I32)
    y = _moe_experts(h, row_tok, blk_e, w13, w2, layer)
    return _moe_combine(x1, wt, y, dest.reshape(t, TOP_K).astype(I32))


def _rope_tables(pos):
    half = ROT_DIM // 2
    inv_freq = ROPE_THETA ** (-jnp.arange(half, dtype=F32) / half)
    ang = pos.astype(F32)[:, None] * inv_freq[None, :]
    cos, sin = jnp.cos(ang), jnp.sin(ang)
    n = pos.shape[0]
    ones = jnp.ones((n, HEAD_DIM - ROT_DIM), F32)
    zeros8 = jnp.zeros((n, half), F32)
    zeros = jnp.zeros((n, HEAD_DIM - ROT_DIM), F32)
    c = jnp.concatenate([cos, cos, ones], axis=1)
    s1 = jnp.concatenate([-sin, zeros8, zeros], axis=1)
    s2 = jnp.concatenate([zeros8, sin, zeros], axis=1)
    return tuple(jnp.tile(a, (1, LANES // HEAD_DIM)) for a in (c, s1, s2))


def _layer_weights(l, p):
    d = p["w_in"].shape[1]
    w_in = p["w_in"][l]
    sizes = (A_W, A_W, A_W, B_QK_W, B_QK_W, B_V_W, C_QK_W, C_QK_W, C_V_W, C_GATE_RANK, C_V_W, M_W, N_BRANCH * d)
    offs = np.concatenate([[0], np.cumsum(sizes)])
    seg = lambda j: w_in[:, int(offs[j]):int(offs[j + 1])]
    ca_pad = jnp.pad(seg(9), ((0, 0), (0, LANES - C_GATE_RANK)))
    w_all = jnp.concatenate([seg(j) for j in (0, 1, 2, 3, 4, 5, 6, 7, 8, 10, 11, 12)] + [ca_pad], axis=1)
    blk = np.kron(np.eye(A_W // HEAD_DIM, dtype=np.float32), np.full((HEAD_DIM, HEAD_DIM), 1.0 / HEAD_DIM, np.float32))
    tile = lambda g, n: jnp.tile(g, n)[None, :].astype(F32)
    w_br = p["w_branch"][l].astype(BF16)
    o1, o2, o3 = A_W, A_W + B_V_W, A_W + B_V_W + C_V_W
    lw = {
        "norm1_g": p["norm1_g"][l][None, :], "w_in": w_all.astype(BF16), "bd": jnp.asarray(blk, BF16),
        "a_qn": tile(p["a_q_norm"][l], A_HEADS), "a_kn": tile(p["a_k_norm"][l], A_HEADS),
        "b_qn": tile(p["b_q_norm"][l], 2 * B_HEADS), "b_kn": tile(p["b_k_norm"][l], 2 * B_HEADS),
        "m_qn": tile(p["m_q_norm"][l], M_HEADS), "m_kn": tile(p["m_k_norm"][l], M_HEADS),
        "c_wa": jnp.pad(p["c_w_alpha"][l], ((0, LANES - C_GATE_RANK), (0, 0))).astype(BF16),
        "c_ba": p["c_b_alpha"][l][None, :],
        "mem_norm": p["mem_norm"][l][None, :], "w_mem_kv": p["w_mem_kv"][l].astype(BF16),
        "w_a": w_br[:o1], "w_b": w_br[o1:o2], "w_c": w_br[o2:o3], "w_m": w_br[o3:],
        "w_o": p["w_out"][l].astype(BF16), "norm2_g": p["norm2_g"][l][None, :],
        "b_lambda": p["b_lambda"][l], "b_subln": p["b_subln"][l][None, :], "c_on": p["c_out_norm"][l][None, :],
        "a_rel_bias": p["a_rel_bias"][l],
    }
    if l % 2 == 0:
        lw["ffn_w13"] = p["ffn_w13"][l // 2].astype(BF16)
        lw["ffn_w2"] = p["ffn_w2"][l // 2].astype(BF16)
    else:
        r = jnp.pad(p["moe_router"][l // 2], ((0, 0), (0, LANES - N_EXPERTS)))
        r_hi = r.astype(BF16)
        lw["router_hi"] = r_hi
        lw["router_lo"] = (r - r_hi.astype(F32)).astype(BF16)
        lw["moe_w13"] = p["moe_w13_bf16"]
        lw["moe_w2"] = p["moe_w2_bf16"]
    return lw


def _channel_mixer(l, x, br, lw):
    if l % 2 == 0:
        return _merge_ffn(x, br, lw)
    x1, h, idx, wt = _merge_router(x, br, lw)
    return _moe(x1, h, idx, wt, lw["moe_w13"], lw["moe_w2"], l // 2)


def kernel(x_prompt, x_sample, cache_a_k, cache_a_v, cache_b_k, cache_b_v, state_c, cache_mem_k, cache_mem_v,
           mem_prompt, norm1_g, w_in, a_q_norm, a_k_norm, a_rel_bias, b_q_norm, b_k_norm, b_lambda, b_subln,
           c_w_alpha, c_b_alpha, c_out_norm, mem_norm, w_mem_kv, m_q_norm, m_k_norm, w_branch, w_out, norm2_g,
           ffn_w13, ffn_w2, moe_router, moe_w13, moe_w2):
    p = dict(norm1_g=norm1_g, w_in=w_in, a_q_norm=a_q_norm, a_k_norm=a_k_norm, a_rel_bias=a_rel_bias,
             b_q_norm=b_q_norm, b_k_norm=b_k_norm, b_lambda=b_lambda, b_subln=b_subln, c_w_alpha=c_w_alpha,
             c_b_alpha=c_b_alpha, c_out_norm=c_out_norm, mem_norm=mem_norm, w_mem_kv=w_mem_kv, m_q_norm=m_q_norm,
             m_k_norm=m_k_norm, w_branch=w_branch, w_out=w_out, norm2_g=norm2_g, ffn_w13=ffn_w13, ffn_w2=ffn_w2,
             moe_router=moe_router, moe_w13_bf16=moe_w13.astype(BF16), moe_w2_bf16=moe_w2.astype(BF16))
    depth = w_in.shape[0]
    nb, seq, d = x_prompt.shape
    sb, t_new, _ = x_sample.shape
    past = cache_b_k.shape[2]
    n_mem = mem_prompt.shape[1]
    a_keep = min(A_PAST, seq)
    assert t_new == CHUNK and cache_a_k.shape[2] == A_PAST and seq % TOKEN_TILE == 0 and (sb * t_new) % TOKEN_TILE == 0

    rope_p = _rope_tables(jnp.tile(jnp.arange(seq), nb))
    rope_s = _rope_tables(jnp.tile(past + jnp.arange(t_new), sb))
    xp = x_prompt.reshape(nb * seq, d)
    xs = x_sample.reshape(sb * t_new, d)
    c_tile = min(C_TILE, seq)
    cache_bk = cache_b_k.reshape(depth, sb, past, B_QK_W)
    cache_bv = cache_b_v.reshape(depth, sb, past, B_V_W)
    outs = {k: [] for k in ("a_kp", "a_vp", "a_ks", "a_vs", "b_kp", "b_vp", "b_ks", "b_vs", "c_sp", "c_ss",
                            "m_kp", "m_vp")}

    for l in range(depth):
        lam_init = 0.8 - 0.6 * math.exp(-0.3 * l)
        lw = _layer_weights(l, p)

        pr = _inproj(xp, lw, rope_p)
        r3 = lambda a, b_=nb, s_=seq: a.reshape(b_, s_, a.shape[-1])
        br = {"gt": pr["gt"]}
        br["oa"] = _band_prompt(r3(pr["aq"]), r3(pr["ak"]), r3(pr["av"]), lw["a_rel_bias"]).reshape(nb * seq, A_W)
        br["ob"] = _diff_prompt(r3(pr["bq"]), r3(pr["bkh"]), r3(pr["bvh"]), lw["b_lambda"], lw["b_subln"],
                                lam_init).reshape(nb * seq, B_V_W)
        oc, c_fin = _gla(r3(pr["cq"]), r3(pr["ck"]), r3(pr["cv"]), r3(pr["cg"]), r3(pr["cr"]),
                         jnp.zeros((nb, C_QK_W, C_DV), F32), lw["c_on"], c_tile)
        br["oc"] = oc.reshape(nb * seq, C_V_W)
        mk, mv = _mem_kv(mem_prompt, lw)
        br["om"] = _dense_attention(r3(pr["mq"]), mk, mv, None, min(M_Q_TILE, seq)).reshape(nb * seq, M_W)
        outs["a_kp"].append(r3(pr["ak"])[:, seq - a_keep:].reshape(nb, a_keep, A_HEADS, HEAD_DIM))
        outs["a_vp"].append(r3(pr["av"])[:, seq - a_keep:].reshape(nb, a_keep, A_HEADS, HEAD_DIM))
        outs["b_kp"].append(pr["bk"].reshape(nb, seq, B_HEADS, 2, HEAD_DIM))
        outs["b_vp"].append(pr["bv"].reshape(nb, seq, B_HEADS, B_VDIM))
        outs["c_sp"].append(c_fin.reshape(nb, C_HEADS, C_DK, C_DV))
        outs["m_kp"].append(mk.reshape(nb, n_mem, M_HEADS, HEAD_DIM))
        outs["m_vp"].append(mv.reshape(nb, n_mem, M_HEADS, HEAD_DIM))
        xp = _channel_mixer(l, xp, br, lw)

        sr = _inproj(xs, lw, rope_s)
        s3 = lambda a: a.reshape(sb, t_new, a.shape[-1])
        bs = {"gt": sr["gt"]}
        ka = jnp.concatenate([cache_a_k[l].reshape(sb, A_PAST, A_W), s3(sr["ak"])], axis=1)
        va = jnp.concatenate([cache_a_v[l].reshape(sb, A_PAST, A_W), s3(sr["av"])], axis=1)
        bias_s = _rel_bias(lw["a_rel_bias"], t_new, A_PAST + t_new, A_PAST)
        bs["oa"] = _dense_attention(s3(sr["aq"]), ka, va, bias_s, t_new).reshape(sb * t_new, A_W)
        bs["ob"] = _diff_sample(s3(sr["bq"]), cache_bk, cache_bv, l, s3(sr["bkh"]), s3(sr["bvh"]),
                                lw["b_lambda"], lw["b_subln"], lam_init).reshape(sb * t_new, B_V_W)
        oc, c_new = _gla(s3(sr["cq"]), s3(sr["ck"]), s3(sr["cv"]), s3(sr["cg"]), s3(sr["cr"]),
                         state_c[l].reshape(sb, C_QK_W, C_DV), lw["c_on"], t_new)
        bs["oc"] = oc.reshape(sb * t_new, C_V_W)
        bs["om"] = _dense_attention(s3(sr["mq"]), cache_mem_k[l].reshape(sb, n_mem, M_W),
                                    cache_mem_v[l].reshape(sb, n_mem, M_W), None, t_new).reshape(sb * t_new, M_W)
        outs["a_ks"].append(sr["ak"].reshape(sb, t_new, A_HEADS, HEAD_DIM))
        outs["a_vs"].append(sr["av"].reshape(sb, t_new, A_HEADS, HEAD_DIM))
        outs["b_ks"].append(sr["bk"].reshape(sb, t_new, B_HEADS, 2, HEAD_DIM))
        outs["b_vs"].append(sr["bv"].reshape(sb, t_new, B_HEADS, B_VDIM))
        outs["c_ss"].append(c_new.reshape(sb, C_HEADS, C_DK, C_DV))
        xs = _channel_mixer(l, xs, bs, lw)

    st = {k: jnp.stack(v) for k, v in outs.items()}
    return (xp.reshape(nb, seq, d), xs.reshape(sb, t_new, d),
            st["a_kp"], st["a_vp"], st["a_ks"], st["a_vs"], st["b_kp"], st["b_vp"], st["b_ks"], st["b_vs"],
            st["c_sp"], st["c_ss"], st["m_kp"], st["m_vp"])
```

```python
import functools
import math

import numpy as np
import jax
import jax.numpy as jnp
from jax import lax
from jax.experimental import pallas as pl
from jax.experimental.pallas import tpu as pltpu

F32 = jnp.float32
BF16 = jnp.bfloat16
I32 = jnp.int32

CHUNK = 64
HEAD_DIM = 64
A_HEADS = 8
A_BAND_CHUNKS = 8
A_PAST = A_BAND_CHUNKS * CHUNK
REL_CLIP = 128
B_HEADS = 4
B_VDIM = 2 * HEAD_DIM
ROT_DIM = HEAD_DIM // 4
ROPE_THETA = 500000.0
C_HEADS = 4
C_DK = 64
C_DV = 128
C_GATE_RANK = 16
C_TAU = 16.0
M_HEADS = 4
N_BRANCH = 4
N_EXPERTS = 8
TOP_K = 2
EPS = 1e-6
NEG_INF = -1e30

A_W = A_HEADS * HEAD_DIM
B_QK_W = B_HEADS * 2 * HEAD_DIM
B_V_W = B_HEADS * B_VDIM
C_QK_W = C_HEADS * C_DK
C_V_W = C_HEADS * C_DV
M_W = M_HEADS * HEAD_DIM

LANES = 128
VMEM_LIMIT_BYTES = 56 * 1024 * 1024

TOKEN_TILE = 256
A_Q_TILE = 256
B_Q_TILE = 1024
B_K_TILE = 1024
B_HEADS_PER_STEP = 2
B_CACHE_TILE = 2048
C_TILE = 256
M_Q_TILE = 512
FF_CHUNK = 256
MOE_ROWS = 256
MOE_FF_CHUNK = 512
QK_SCALE = HEAD_DIM ** -0.5
LOG2_E = math.log2(math.e)

_SEG = {}
_off = 0
for _name, _w in (("aq", A_W), ("ak", A_W), ("av", A_W), ("bq", B_QK_W), ("bk", B_QK_W), ("bv", B_V_W),
                  ("cq", C_QK_W), ("ck", C_QK_W), ("cv", C_V_W), ("cr", C_V_W), ("mq", M_W)):
    _SEG[_name] = (_off, _w)
    _off += _w
GATE_OFF = _off


def _cparams(*sem):
    return pltpu.CompilerParams(dimension_semantics=sem, vmem_limit_bytes=VMEM_LIMIT_BYTES)


def _const_spec(shape):
    nd = len(shape)
    return pl.BlockSpec(shape, lambda *_: (0,) * nd, pipeline_mode=pl.Buffered(1))


def _dot(a, b):
    return jnp.dot(a, b, preferred_element_type=F32)


def _dot_nt(a, b):
    return lax.dot_general(a, b, (((1,), (1,)), ((), ())), preferred_element_type=F32)


def _dot_tn(a, b):
    return lax.dot_general(a, b, (((0,), (0,)), ((), ())), preferred_element_type=F32)


def _rms(x, gain):
    return x * lax.rsqrt(jnp.mean(x * x, axis=-1, keepdims=True) + EPS) * gain


def _head_norm(y, bd_ref, gain):
    w = y.shape[-1]
    ms = _dot((y * y).astype(BF16), bd_ref[:w, :w])
    return y * lax.rsqrt(ms + EPS) * gain


def _sigmoid(x):
    return 1.0 / (1.0 + jnp.exp(-x))


def _inproj_body(x_ref, g1_ref, w_ref, bd_ref, aqn_ref, akn_ref, bqn_ref, bkn_ref, mqn_ref, cwa_ref, cba_ref,
                 rc_ref, rs1_ref, rs2_ref,
                 aq_o, ak_o, av_o, bq_o, bk_o, bkh_o, bv_o, bvh_o, cq_o, ck_o, cv_o, cg_o, cr_o, mq_o, gt_o):
    x = x_ref[...]
    h = _rms(x, g1_ref[...]).astype(BF16)
    d_model = x.shape[-1]

    def proj(name):
        lo, w = _SEG[name]
        return _dot(h, w_ref[:, lo:lo + w])

    def rope(y):
        reps = y.shape[-1] // LANES
        c = jnp.concatenate([rc_ref[...]] * reps, axis=1)
        s1 = jnp.concatenate([rs1_ref[...]] * reps, axis=1)
        s2 = jnp.concatenate([rs2_ref[...]] * reps, axis=1)
        half = ROT_DIM // 2
        return y * c + pltpu.roll(y, y.shape[-1] - half, 1) * s1 + pltpu.roll(y, half, 1) * s2

    aq_o[...] = (_head_norm(proj("aq"), bd_ref, aqn_ref[...]) * QK_SCALE).astype(BF16)
    ak_o[...] = _head_norm(proj("ak"), bd_ref, akn_ref[...])
    av_o[...] = proj("av")
    bq_o[...] = (rope(_head_norm(proj("bq"), bd_ref, bqn_ref[...])) * (QK_SCALE * LOG2_E)).astype(BF16)
    bk = rope(_head_norm(proj("bk"), bd_ref, bkn_ref[...]))
    bk_o[...] = bk
    bkh_o[...] = bk.astype(BF16)
    bv = proj("bv")
    bv_o[...] = bv
    bvh_o[...] = bv.astype(BF16)
    cq_o[...] = proj("cq") * (C_DK ** -0.5)
    ck_o[...] = proj("ck")
    cv_o[...] = proj("cv").astype(BF16)
    cr_o[...] = proj("cr").astype(BF16)
    mq_o[...] = (_head_norm(proj("mq"), bd_ref, mqn_ref[...]) * QK_SCALE).astype(BF16)
    ca = _dot(h, w_ref[:, GATE_OFF + N_BRANCH * d_model:])
    z = _dot(ca.astype(BF16), cwa_ref[...]) + cba_ref[...]
    cg_o[...] = (jnp.minimum(z, 0.0) - jnp.log1p(jnp.exp(-jnp.abs(z)))) * (1.0 / C_TAU)
    for c in range(N_BRANCH):
        lo = GATE_OFF + c * d_model
        gt_o[:, c * d_model:(c + 1) * d_model] = _sigmoid(_dot(h, w_ref[:, lo:lo + d_model])).astype(BF16)


def _inproj(x, lw, rope_tabs):
    t, d = x.shape
    tm = TOKEN_TILE
    wcols = lw["w_in"].shape[1]
    row = lambda w: pl.BlockSpec((tm, w), lambda i: (i, 0))
    outs = [("aq", A_W, BF16), ("ak", A_W, F32), ("av", A_W, F32), ("bq", B_QK_W, BF16), ("bk", B_QK_W, F32),
            ("bkh", B_QK_W, BF16), ("bv", B_V_W, F32), ("bvh", B_V_W, BF16), ("cq", C_QK_W, F32),
            ("ck", C_QK_W, F32), ("cv", C_V_W, BF16), ("cg", C_QK_W, F32), ("cr", C_V_W, BF16),
            ("mq", M_W, BF16), ("gt", N_BRANCH * d, BF16)]
    res = pl.pallas_call(
        _inproj_body,
        grid=(t // tm,),
        in_specs=[row(d), _const_spec((1, d)), _const_spec((d, wcols)), _const_spec((A_W, A_W)),
                  _const_spec((1, A_W)), _const_spec((1, A_W)), _const_spec((1, B_QK_W)), _const_spec((1, B_QK_W)),
                  _const_spec((1, M_W)), _const_spec((LANES, C_QK_W)), _const_spec((1, C_QK_W)),
                  row(LANES), row(LANES), row(LANES)],
        out_specs=[row(w) for _, w, _ in outs],
        out_shape=[jax.ShapeDtypeStruct((t, w), dt) for _, w, dt in outs],
        compiler_params=_cparams("parallel"),
        name="inproj",
    )(x, lw["norm1_g"], lw["w_in"], lw["bd"], lw["a_qn"], lw["a_kn"], lw["b_qn"], lw["b_kn"], lw["m_qn"],
      lw["c_wa"], lw["c_ba"], *rope_tabs)
    return {n: r for (n, _, _), r in zip(outs, res)}


def _pair_attention(q2, k2, v2, bias_even, bias_odd):
    tq = q2.shape[0]
    lane = lax.broadcasted_iota(I32, q2.shape, 1)
    zero = jnp.zeros_like(q2)
    qq = jnp.concatenate([jnp.where(lane < HEAD_DIM, q2, zero), jnp.where(lane >= HEAD_DIM, q2, zero)], axis=0)
    s = _dot_nt(qq, k2)
    if bias_even is not None:
        s = s + jnp.concatenate([bias_even, bias_odd], axis=0)
    m = jnp.max(s, axis=-1, keepdims=True)
    p = jnp.exp(s - m)
    l = jnp.sum(p, axis=-1, keepdims=True)
    o = _dot(p.astype(BF16), v2) / l
    return jnp.where(lane < HEAD_DIM, o[:tq], o[tq:])


def _band_body(q_ref, k0_ref, k1_ref, k2_ref, v0_ref, v1_ref, v2_ref, bias_ref, o_ref):
    i = pl.program_id(1)
    tq = q_ref.shape[1]
    k = jnp.concatenate([k0_ref[0], k1_ref[0], k2_ref[0]], axis=0).astype(BF16)
    v = jnp.concatenate([v0_ref[0], v1_ref[0], v2_ref[0]], axis=0).astype(BF16)
    col = lax.broadcasted_iota(I32, (tq, 3 * tq), 1)
    pad = jnp.where(col >= 2 * tq - i * tq, 0.0, NEG_INF)
    for p in range(A_HEADS // 2):
        sl = slice(p * LANES, (p + 1) * LANES)
        o_ref[0, :, sl] = _pair_attention(q_ref[0, :, sl], k[:, sl], v[:, sl],
                                          bias_ref[2 * p] + pad, bias_ref[2 * p + 1] + pad).astype(o_ref.dtype)


def _rel_bias(table, n_q, n_k, offset):
    period = n_q + n_k
    k = np.arange(period)
    dist = np.where(k < n_k, offset - k, offset - k + period)
    w = table[:, np.clip(dist, -REL_CLIP, REL_CLIP) + REL_CLIP].astype(F32)
    heads = table.shape[0]
    return jnp.tile(w, (1, n_q))[:, :n_q * (period - 1)].reshape(heads, n_q, period - 1)[:, :, :n_k]


def _band_bias(table, tq):
    r = np.arange(tq)[:, None]
    c = np.arange(3 * tq)[None, :]
    qc = (r + 2 * tq) // CHUNK
    kc = c // CHUNK
    vis = (kc <= qc) & (kc >= qc - A_BAND_CHUNKS)
    return jnp.where(jnp.asarray(vis)[None], _rel_bias(table, tq, 3 * tq, 2 * tq), NEG_INF)


def _band_prompt(aq, ak, av, table):
    b, s, w = aq.shape
    tq = A_Q_TILE
    assert A_PAST == 2 * tq and s % tq == 0
    qspec = pl.BlockSpec((1, tq, w), lambda bi, i: (bi, i, 0))
    kspec = lambda back: pl.BlockSpec((1, tq, w), lambda bi, i: (bi, jnp.maximum(i - back, 0), 0))
    return pl.pallas_call(
        _band_body,
        grid=(b, s // tq),
        in_specs=[qspec, kspec(2), kspec(1), kspec(0), kspec(2), kspec(1), kspec(0),
                  _const_spec((A_HEADS, tq, 3 * tq))],
        out_specs=qspec,
        out_shape=jax.ShapeDtypeStruct((b, s, w), BF16),
        compiler_params=_cparams("parallel", "parallel"),
        name="band_attention",
    )(aq, ak, ak, ak, av, av, av, _band_bias(table, tq))


def _dense_attn_body(*refs, heads, has_bias):
    if has_bias:
        q_ref, k_ref, v_ref, bias_ref, o_ref = refs
    else:
        q_ref, k_ref, v_ref, o_ref = refs
    k = k_ref[0].astype(BF16)
    v = v_ref[0].astype(BF16)
    for p in range(heads // 2):
        sl = slice(p * LANES, (p + 1) * LANES)
        be, bo = (bias_ref[2 * p], bias_ref[2 * p + 1]) if has_bias else (None, None)
        o_ref[0, :, sl] = _pair_attention(q_ref[0, :, sl], k[:, sl], v[:, sl], be, bo).astype(o_ref.dtype)


def _dense_attention(q, k, v, bias, tq):
    b, sq, w = q.shape
    sk = k.shape[1]
    heads = w // HEAD_DIM
    qspec = pl.BlockSpec((1, tq, w), lambda bi, i: (bi, i, 0))
    kspec = pl.BlockSpec((1, sk, w), lambda bi, i: (bi, 0, 0))
    in_specs = [qspec, kspec, kspec]
    args = [q, k, v]
    if bias is not None:
        in_specs.append(_const_spec(bias.shape))
        args.append(bias)
    return pl.pallas_call(
        functools.partial(_dense_attn_body, heads=heads, has_bias=bias is not None),
        grid=(b, sq // tq),
        in_specs=in_specs,
        out_specs=qspec,
        out_shape=jax.ShapeDtypeStruct((b, sq, w), BF16),
        compiler_params=_cparams("parallel", "parallel"),
        name="dense_attention",
    )(*args)


def _memkv_body(mem_ref, g_ref, w_ref, bd_ref, kn_ref, k_o, v_o):
    h = _rms(mem_ref[0], g_ref[...]).astype(BF16)
    kv = _dot(h, w_ref[...])
    k_o[0] = _head_norm(kv[:, :M_W], bd_ref, kn_ref[...])
    v_o[0] = kv[:, M_W:]


def _mem_kv(mem, lw):
    b, n, d = mem.shape
    spec = pl.BlockSpec((1, n, M_W), lambda bi: (bi, 0, 0))
    return pl.pallas_call(
        _memkv_body,
        grid=(b,),
        in_specs=[pl.BlockSpec((1, n, d), lambda bi: (bi, 0, 0)), _const_spec((1, d)), _const_spec((d, 2 * M_W)),
                  _const_spec((A_W, A_W)), _const_spec((1, M_W))],
        out_specs=[spec, spec],
        out_shape=[jax.ShapeDtypeStruct((b, n, M_W), F32)] * 2,
        compiler_params=_cparams("parallel"),
        name="mem_kv",
    )(mem, lw["mem_norm"], lw["w_mem_kv"], lw["bd"], lw["m_kn"])


def _diff_init(q, qq_scr, m_scr, l_scr, acc_scr):
    lane = lax.broadcasted_iota(I32, q.shape, 1)
    zero = jnp.zeros_like(q)
    tq = q.shape[0]
    qq_scr[:tq] = jnp.where(lane < HEAD_DIM, q, zero)
    qq_scr[tq:] = jnp.where(lane >= HEAD_DIM, q, zero)
    m_scr[...] = jnp.full(m_scr.shape, NEG_INF, F32)
    l_scr[...] = jnp.zeros(l_scr.shape, F32)
    acc_scr[...] = jnp.zeros(acc_scr.shape, F32)


def _diff_step(k, v, qq_scr, m_scr, l_scr, acc_scr, visible):
    s = _dot_nt(qq_scr[...], k)
    if visible is not None:
        s = jnp.where(visible, s, NEG_INF)
    m_prev = m_scr[...]
    m_new = jnp.maximum(m_prev, jnp.max(s, axis=-1, keepdims=True))
    alpha = jnp.exp2(m_prev - m_new)
    tk = s.shape[-1]
    m_wide = m_new[:, :tk] if tk <= LANES else jnp.concatenate([m_new] * (tk // LANES), axis=1)
    p = jnp.exp2(s - m_wide)
    l_scr[...] = alpha * l_scr[...] + jnp.sum(p, axis=-1, keepdims=True)
    acc_scr[...] = alpha * acc_scr[...] + _dot(p.astype(BF16), v)
    m_scr[...] = m_new


def _diff_finish(lam_ref, sub_ref, l_scr, acc_scr, lam_init):
    tq = acc_scr.shape[0] // 2
    lam_p = lam_ref[...]
    lam = (jnp.exp(jnp.sum(lam_p[0:1] * lam_p[1:2], axis=-1, keepdims=True))
           - jnp.exp(jnp.sum(lam_p[2:3] * lam_p[3:4], axis=-1, keepdims=True)) + lam_init)
    o = acc_scr[:tq] / l_scr[:tq] - lam * (acc_scr[tq:] / l_scr[tq:])
    return _rms(o, sub_ref[...]) * (1.0 - lam_init)


def _diff_scratch(tq):
    g = B_HEADS_PER_STEP
    return [pltpu.VMEM((g, 2 * tq, LANES), BF16), pltpu.VMEM((g, 2 * tq, LANES), F32),
            pltpu.VMEM((g, 2 * tq, LANES), F32), pltpu.VMEM((g, 2 * tq, B_VDIM), F32)]


def _head_lanes(g):
    return slice(g * LANES, (g + 1) * LANES)


def _diff_prompt_body(qi_ref, ki_ref, q_ref, k_ref, v_ref, lam_ref, sub_ref, o_ref,
                      qq_scr, m_scr, l_scr, acc_scr, *, lam_init):
    s_id = pl.program_id(2)
    qi = qi_ref[s_id]
    ki = ki_ref[s_id]
    tq = q_ref.shape[1]
    tk = k_ref.shape[1]
    heads = range(B_HEADS_PER_STEP)
    state = lambda g: (qq_scr.at[g], m_scr.at[g], l_scr.at[g], acc_scr.at[g])

    @pl.when(ki == 0)
    def _():
        for g in heads:
            _diff_init(q_ref[0, :, _head_lanes(g)], *state(g))

    needs_mask = (ki + 1) * tk > qi * tq

    @pl.when(needs_mask)
    def _():
        row = lax.broadcasted_iota(I32, (2 * tq, tk), 0)
        row = jnp.where(row >= tq, row - tq, row)
        col = lax.broadcasted_iota(I32, (2 * tq, tk), 1)
        chunk_shift = int(math.log2(CHUNK))
        visible = ((ki * tk + col) >> chunk_shift) <= ((qi * tq + row) >> chunk_shift)
        for g in heads:
            _diff_step(k_ref[0, :, _head_lanes(g)], v_ref[0, :, _head_lanes(g)], *state(g), visible)

    @pl.when(jnp.logical_not(needs_mask))
    def _():
        for g in heads:
            _diff_step(k_ref[0, :, _head_lanes(g)], v_ref[0, :, _head_lanes(g)], *state(g), None)

    last_ki = ((qi + 1) * tq - 1) // tk

    @pl.when(ki == last_ki)
    def _():
        for g in heads:
            o_ref[0, :, _head_lanes(g)] = _diff_finish(lam_ref, sub_ref, l_scr.at[g], acc_scr.at[g],
                                                       lam_init).astype(o_ref.dtype)


def _diff_prompt(bq, bkh, bvh, lam_p, sub, lam_init):
    b, s, w = bq.shape
    tq = min(B_Q_TILE, s)
    tk = min(B_K_TILE, s)
    gw = B_HEADS_PER_STEP * LANES
    steps = [(qi, ki) for qi in range(s // tq) for ki in range(((qi + 1) * tq - 1) // tk + 1)]
    qi_tab = jnp.asarray([p[0] for p in steps], I32)
    ki_tab = jnp.asarray([p[1] for p in steps], I32)
    qspec = pl.BlockSpec((1, tq, gw), lambda bi, h, st, qt, kt: (bi, qt[st], h))
    kspec = pl.BlockSpec((1, tk, gw), lambda bi, h, st, qt, kt: (bi, kt[st], h))
    cspec = lambda shp: pl.BlockSpec(shp, lambda bi, h, st, qt, kt: (0, 0))
    return pl.pallas_call(
        functools.partial(_diff_prompt_body, lam_init=lam_init),
        grid_spec=pltpu.PrefetchScalarGridSpec(
            num_scalar_prefetch=2,
            grid=(b, B_HEADS // B_HEADS_PER_STEP, len(steps)),
            in_specs=[qspec, kspec, kspec, cspec((4, HEAD_DIM)), cspec((1, B_VDIM))],
            out_specs=qspec,
            scratch_shapes=_diff_scratch(tq)),
        out_shape=jax.ShapeDtypeStruct((b, s, w), BF16),
        compiler_params=_cparams("parallel", "parallel", "arbitrary"),
        name="diff_attention",
    )(qi_tab, ki_tab, bq, bkh, bvh, lam_p, sub)


def _diff_sample_body(q_ref, ck_ref, cv_ref, nk_ref, nv_ref, lam_ref, sub_ref, o_ref,
                      qq_scr, m_scr, l_scr, acc_scr, *, lam_init):
    t = pl.program_id(2)
    heads = range(B_HEADS_PER_STEP)
    state = lambda g: (qq_scr.at[g], m_scr.at[g], l_scr.at[g], acc_scr.at[g])

    @pl.when(t == 0)
    def _():
        for g in heads:
            _diff_init(q_ref[0, :, _head_lanes(g)], *state(g))

    for g in heads:
        _diff_step(ck_ref[0, :, _head_lanes(g)].astype(BF16), cv_ref[0, :, _head_lanes(g)].astype(BF16),
                   *state(g), None)

    @pl.when(t == pl.num_programs(2) - 1)
    def _():
        for g in heads:
            _diff_step(nk_ref[0, :, _head_lanes(g)], nv_ref[0, :, _head_lanes(g)], *state(g), None)
            o_ref[0, :, _head_lanes(g)] = _diff_finish(lam_ref, sub_ref, l_scr.at[g], acc_scr.at[g],
                                                       lam_init).astype(o_ref.dtype)


def _diff_sample(bq, cache_k, cache_v, layer, bkh, bvh, lam_p, sub, lam_init):
    b, tq, w = bq.shape
    past = cache_k.shape[2]
    tc = min(B_CACHE_TILE, past)
    gw = B_HEADS_PER_STEP * LANES
    assert past % tc == 0 and tq == CHUNK and past % CHUNK == 0
    qspec = pl.BlockSpec((1, tq, gw), lambda bi, h, t: (bi, 0, h))
    cspec = pl.BlockSpec((None, 1, tc, gw), lambda bi, h, t: (layer, bi, t, h))
    pspec = lambda shp: pl.BlockSpec(shp, lambda bi, h, t: (0, 0))
    return pl.pallas_call(
        functools.partial(_diff_sample_body, lam_init=lam_init),
        grid=(b, B_HEADS // B_HEADS_PER_STEP, past // tc),
        in_specs=[qspec, cspec, cspec, qspec, qspec, pspec((4, HEAD_DIM)), pspec((1, B_VDIM))],
        out_specs=qspec,
        scratch_shapes=_diff_scratch(tq),
        out_shape=jax.ShapeDtypeStruct((b, tq, w), BF16),
        compiler_params=_cparams("parallel", "parallel", "arbitrary"),
        name="diff_attention_sample",
    )(bq, cache_k, cache_v, bkh, bvh, lam_p, sub)


def _gla_decay_matrix(t):
    levels = int(math.log2(t))
    d = np.zeros((levels + 2, t, t), np.float32)
    u = np.arange(t)[None, :]
    i = np.arange(t)[:, None]
    for lv in range(levels):
        g = t >> lv
        base = i - i % g
        r = base + g // 2 - 1
        second = (i % g) >= g // 2
        d[lv] = np.where(second, (u > r) & (u <= i), (u > i) & (u <= r))
    d[levels] = u <= i
    d[levels + 1] = u > i
    return jnp.asarray(d.reshape((levels + 2) * t, t), BF16)


def _gla_body(q_ref, k_ref, v_ref, g_ref, r_ref, s0_ref, d_ref, on_ref, o_ref, sf_ref, s_scr):
    c = pl.program_id(1)
    t = q_ref.shape[1]
    levels = int(math.log2(t))

    @pl.when(c == 0)
    def _():
        s_scr[...] = s0_ref[0]

    q = q_ref[0]
    k = k_ref[0]
    g = g_ref[0]
    g_hi = g.astype(BF16)
    g_lo = (g - g_hi.astype(F32)).astype(BF16)
    g2 = jnp.concatenate([g_hi, g_lo], axis=1)

    def exponent(block):
        e = _dot(d_ref[block * t:(block + 1) * t, :], g2)
        return e[:, :C_QK_W] + e[:, C_QK_W:]

    lane = lax.broadcasted_iota(I32, (t, C_QK_W), 1)
    head_lanes = [(lane >= h * C_DK) & (lane < (h + 1) * C_DK) for h in range(C_HEADS)]
    tok = lax.broadcasted_iota(I32, (t, C_QK_W), 0)
    ri = lax.broadcasted_iota(I32, (t, t), 0)
    ci = lax.broadcasted_iota(I32, (t, t), 1)
    zero = jnp.zeros((t, C_QK_W), BF16)
    kb = k.astype(BF16)
    qb = q.astype(BF16)

    att = [jnp.where(ri == ci, _dot_nt(jnp.where(head_lanes[h], qb, zero), kb), 0.0) for h in range(C_HEADS)]
    for lv in range(levels):
        shift = levels - lv
        w = jnp.exp(exponent(lv))
        second = ((tok >> (shift - 1)) & 1) == 1
        ql = jnp.where(second, q * w, 0.0).astype(BF16)
        kl = jnp.where(second, 0.0, k * w).astype(BF16)
        same = (ri >> shift) == (ci >> shift)
        for h in range(C_HEADS):
            att[h] = att[h] + jnp.where(same, _dot_nt(jnp.where(head_lanes[h], ql, zero), kl), 0.0)

    state = s_scr[...]
    state_b = state.astype(BF16)
    q_in = (q * jnp.exp(exponent(levels))).astype(BF16)
    k_out = (k * jnp.exp(exponent(levels + 1))).astype(BF16)
    ones = jnp.ones((t, C_DV), BF16)
    decay = jnp.exp(_dot_tn(g_hi, ones) + _dot_tn(g_lo, ones))
    srow = lax.broadcasted_iota(I32, (C_QK_W, C_DV), 0)
    new_state = decay * state
    for h in range(C_HEADS):
        sl = slice(h * C_DV, (h + 1) * C_DV)
        v = v_ref[0, :, sl]
        o = _dot(att[h].astype(BF16), v) + _dot(jnp.where(head_lanes[h], q_in, zero), state_b)
        r = r_ref[0, :, sl].astype(F32)
        o_ref[0, :, sl] = (_rms(o, on_ref[...]) * (r * _sigmoid(r))).astype(o_ref.dtype)
        upd = _dot_tn(k_out, v)
        new_state = new_state + jnp.where((srow >= h * C_DK) & (srow < (h + 1) * C_DK), upd, 0.0)
    s_scr[...] = new_state

    @pl.when(c == pl.num_programs(1) - 1)
    def _():
        sf_ref[0] = new_state


def _gla(cq, ck, cv, cg, cr, state0, c_on, t):
    b, s, _ = cq.shape
    levels = int(math.log2(t))
    assert 1 << levels == t and s % t == 0
    tok = lambda w: pl.BlockSpec((1, t, w), lambda bi, c: (bi, c, 0))
    sspec = pl.BlockSpec((1, C_QK_W, C_DV), lambda bi, c: (bi, 0, 0))
    return pl.pallas_call(
        _gla_body,
        grid=(b, s // t),
        in_specs=[tok(C_QK_W), tok(C_QK_W), tok(C_V_W), tok(C_QK_W), tok(C_V_W), sspec,
                  _const_spec(((levels + 2) * t, t)), _const_spec((1, C_DV))],
        out_specs=[tok(C_V_W), sspec],
        out_shape=[jax.ShapeDtypeStruct((b, s, C_V_W), BF16), jax.ShapeDtypeStruct((b, C_QK_W, C_DV), F32)],
        scratch_shapes=[pltpu.VMEM((C_QK_W, C_DV), F32)],
        compiler_params=_cparams("parallel", "arbitrary"),
        name="gla",
    )(cq, ck, cv, cg, cr, state0, _gla_decay_matrix(t), c_on)


def _merge_residual(x_ref, oa_ref, ob_ref, oc_ref, om_ref, gt_ref, wa_ref, wb_ref, wc_ref, wm_ref, wo_ref):
    d = x_ref.shape[-1]
    gate = lambda c: gt_ref[:, c * d:(c + 1) * d].astype(F32)
    y = (gate(0) * _dot(oa_ref[...], wa_ref[...]) + gate(1) * _dot(ob_ref[...], wb_ref[...])
         + gate(2) * _dot(oc_ref[...], wc_ref[...]) + gate(3) * _dot(om_ref[...], wm_ref[...]))
    return x_ref[...] + _dot(y.astype(BF16), wo_ref[...])


def _merge_ffn_body(x_ref, oa_ref, ob_ref, oc_ref, om_ref, gt_ref, wa_ref, wb_ref, wc_ref, wm_ref, wo_ref,
                    g2_ref, w13_ref, w2_ref, o_ref):
    x1 = _merge_residual(x_ref, oa_ref, ob_ref, oc_ref, om_ref, gt_ref, wa_ref, wb_ref, wc_ref, wm_ref, wo_ref)
    h = _rms(x1, g2_ref[...]).astype(BF16)
    d_ff = w2_ref.shape[0]
    acc = x1
    for c in range(d_ff // FF_CHUNK):
        lo = c * FF_CHUNK
        a = _dot(h, w13_ref[:, lo:lo + FF_CHUNK])
        b = _dot(h, w13_ref[:, d_ff + lo:d_ff + lo + FF_CHUNK])
        acc = acc + _dot((a * _sigmoid(a) * b).astype(BF16), w2_ref[lo:lo + FF_CHUNK, :])
    o_ref[...] = acc


def _merge_router_body(x_ref, oa_ref, ob_ref, oc_ref, om_ref, gt_ref, wa_ref, wb_ref, wc_ref, wm_ref, wo_ref,
                       g2_ref, rh_ref, rl_ref, x1_o, h_o, idx_o, wt_o):
    x1 = _merge_residual(x_ref, oa_ref, ob_ref, oc_ref, om_ref, gt_ref, wa_ref, wb_ref, wc_ref, wm_ref, wo_ref)
    x1_o[...] = x1
    h = _rms(x1, g2_ref[...])
    h_o[...] = h
    h_hi = h.astype(BF16)
    h_lo = (h - h_hi.astype(F32)).astype(BF16)
    logits = _dot(h_hi, rh_ref[...]) + _dot(h_lo, rh_ref[...]) + _dot(h_hi, rl_ref[...])
    lane = lax.broadcasted_iota(I32, logits.shape, 1)
    lane_f = lane.astype(F32)
    neg = jnp.float32(-jnp.inf)
    lg = jnp.where(lane < N_EXPERTS, logits, neg)
    m1 = jnp.max(lg, axis=-1, keepdims=True)
    i1 = jnp.min(jnp.where(lg == m1, lane_f, float(LANES)), axis=-1, keepdims=True)
    lg2 = jnp.where(lane_f == i1, neg, lg)
    m2 = jnp.max(lg2, axis=-1, keepdims=True)
    i2 = jnp.min(jnp.where(lg2 == m2, lane_f, float(LANES)), axis=-1, keepdims=True)
    e = jnp.exp(m2 - m1)
    w1 = 1.0 / (1.0 + e)
    idx_o[...] = jnp.where(lane == 0, i1, jnp.where(lane == 1, i2, 0.0)).astype(I32)
    wt_o[...] = jnp.where(lane == 0, w1, jnp.where(lane == 1, e * w1, 0.0))


def _merge_specs(t, d):
    tm = TOKEN_TILE
    row = lambda w: pl.BlockSpec((tm, w), lambda i: (i, 0))
    specs = [row(d), row(A_W), row(B_V_W), row(C_V_W), row(M_W), row(N_BRANCH * d),
             _const_spec((A_W, d)), _const_spec((B_V_W, d)), _const_spec((C_V_W, d)), _const_spec((M_W, d)),
             _const_spec((d, d)), _const_spec((1, d))]
    return tm, row, specs


def _merge_args(x, br, lw):
    return (x, br["oa"], br["ob"], br["oc"], br["om"], br["gt"], lw["w_a"], lw["w_b"], lw["w_c"], lw["w_m"],
            lw["w_o"], lw["norm2_g"])


def _merge_ffn(x, br, lw):
    t, d = x.shape
    tm, row, specs = _merge_specs(t, d)
    d_ff = lw["ffn_w2"].shape[0]
    assert d_ff % FF_CHUNK == 0
    return pl.pallas_call(
        _merge_ffn_body,
        grid=(t // tm,),
        in_specs=specs + [_const_spec((d, 2 * d_ff)), _const_spec((d_ff, d))],
        out_specs=row(d),
        out_shape=jax.ShapeDtypeStruct((t, d), F32),
        compiler_params=_cparams("parallel"),
        name="merge_ffn",
    )(*_merge_args(x, br, lw), lw["ffn_w13"], lw["ffn_w2"])


def _merge_router(x, br, lw):
    t, d = x.shape
    tm, row, specs = _merge_specs(t, d)
    return pl.pallas_call(
        _merge_router_body,
        grid=(t // tm,),
        in_specs=specs + [_const_spec((d, LANES)), _const_spec((d, LANES))],
        out_specs=[row(d), row(d), row(LANES), row(LANES)],
        out_shape=[jax.ShapeDtypeStruct((t, d), F32), jax.ShapeDtypeStruct((t, d), F32),
                   jax.ShapeDtypeStruct((t, LANES), I32), jax.ShapeDtypeStruct((t, LANES), F32)],
        compiler_params=_cparams("parallel"),
        name="merge_router",
    )(*_merge_args(x, br, lw), lw["router_hi"], lw["router_lo"])


def _row_gather_copy(src_hbm, idx_ref, n, dst, sem):
    def issue(r, carry):
        pltpu.make_async_copy(src_hbm.at[pl.ds(idx_ref[0, 0, r], 1)], dst.at[pl.ds(r, 1)], sem).start()
        return carry
    lax.fori_loop(0, n, issue, 0, unroll=8)


def _row_gather_copy_unrolled(src_hbm, idx_ref, lo, hi, dst, sem):
    for r in range(lo, hi):
        pltpu.make_async_copy(src_hbm.at[pl.ds(idx_ref[0, 0, r], 1)], dst.at[pl.ds(r, 1)], sem).start()


def _row_gather_wait(dst, sem):
    pltpu.make_async_copy(dst, dst, sem).wait()


def _moe_body(blk_e_ref, cur_ref, nxt_ref, h_hbm, w13_ref, w2_ref, y_ref, xbuf, sem):
    i = pl.program_id(0)
    n = pl.num_programs(0)
    rows = xbuf.shape[1]
    slot = i % 2

    @pl.when(i == 0)
    def _():
        _row_gather_copy(h_hbm, cur_ref, rows, xbuf.at[0], sem.at[0])

    _row_gather_wait(xbuf.at[slot], sem.at[slot])
    x = xbuf[slot].astype(BF16)
    d_ff = w2_ref.shape[1]
    n_chunks = d_ff // MOE_FF_CHUNK
    n_groups = 3 * n_chunks
    bounds = [rows * g // n_groups for g in range(n_groups + 1)]

    def prefetch(g):
        _row_gather_copy_unrolled(h_hbm, nxt_ref, bounds[g], bounds[g + 1], xbuf.at[1 - slot], sem.at[1 - slot])

    acc = jnp.zeros(y_ref.shape, F32)
    for c in range(n_chunks):
        lo = c * MOE_FF_CHUNK
        a = _dot(x, w13_ref[0, :, lo:lo + MOE_FF_CHUNK])
        prefetch(3 * c)
        b = _dot(x, w13_ref[0, :, d_ff + lo:d_ff + lo + MOE_FF_CHUNK])
        prefetch(3 * c + 1)
        acc = acc + _dot((a * _sigmoid(a) * b).astype(BF16), w2_ref[0, lo:lo + MOE_FF_CHUNK, :])
        prefetch(3 * c + 2)
    y_ref[...] = acc

    @pl.when(i == n - 1)
    def _():
        _row_gather_wait(xbuf.at[1 - slot], sem.at[1 - slot])


def _moe_experts(h, row_tok, blk_e, w13, w2, layer):
    t, d = h.shape
    n_blocks = blk_e.shape[0]
    rows = MOE_ROWS
    d_ff = w2.shape[2]
    assert d_ff % MOE_FF_CHUNK == 0
    tok3 = row_tok.reshape(n_blocks, 1, rows)
    smem = lambda f: pl.BlockSpec((1, 1, rows), f, memory_space=pltpu.SMEM)
    return pl.pallas_call(
        _moe_body,
        grid_spec=pltpu.PrefetchScalarGridSpec(
            num_scalar_prefetch=1,
            grid=(n_blocks,),
            in_specs=[smem(lambda i, be: (i, 0, 0)),
                      smem(lambda i, be: (jnp.minimum(i + 1, n_blocks - 1), 0, 0)),
                      pl.BlockSpec(memory_space=pl.ANY),
                      pl.BlockSpec((None, 1, d, 2 * d_ff), lambda i, be: (layer, be[i], 0, 0)),
                      pl.BlockSpec((None, 1, d_ff, d), lambda i, be: (layer, be[i], 0, 0))],
            out_specs=pl.BlockSpec((rows, d), lambda i, be: (i, 0)),
            scratch_shapes=[pltpu.VMEM((2, rows, d), F32), pltpu.SemaphoreType.DMA((2,))]),
        out_shape=jax.ShapeDtypeStruct((n_blocks * rows, d), F32),
        compiler_params=_cparams("arbitrary"),
        name="moe_experts",
    )(blk_e, tok3, tok3, h, w13, w2)


def _combine_body(cur_ref, nxt_ref, x_ref, wt_ref, y_hbm, o_ref, ybuf, sem):
    i = pl.program_id(0)
    n = pl.num_programs(0)
    tm = x_ref.shape[0]
    slot = i % 2

    @pl.when(i == 0)
    def _():
        _row_gather_copy(y_hbm, cur_ref, 2 * tm, ybuf.at[0], sem.at[0])

    _row_gather_copy_unrolled(y_hbm, nxt_ref, 0, 2 * tm, ybuf.at[1 - slot], sem.at[1 - slot])
    _row_gather_wait(ybuf.at[slot], sem.at[slot])
    wt = wt_ref[...]
    o_ref[...] = x_ref[...] + wt[:, 0:1] * ybuf[slot, :tm] + wt[:, 1:2] * ybuf[slot, tm:]

    @pl.when(i == n - 1)
    def _():
        _row_gather_wait(ybuf.at[1 - slot], sem.at[1 - slot])


def _moe_combine(x1, wt, y, dest):
    t, d = x1.shape
    tm = TOKEN_TILE
    n = t // tm
    dest3 = dest.reshape(n, tm, 2).transpose(0, 2, 1).reshape(n, 1, 2 * tm)
    smem = lambda f: pl.BlockSpec((1, 1, 2 * tm), f, memory_space=pltpu.SMEM)
    row = lambda w: pl.BlockSpec((tm, w), lambda i: (i, 0))
    return pl.pallas_call(
        _combine_body,
        grid=(n,),
        in_specs=[smem(lambda i: (i, 0, 0)), smem(lambda i: (jnp.minimum(i + 1, n - 1), 0, 0)),
                  row(d), row(LANES), pl.BlockSpec(memory_space=pl.ANY)],
        out_specs=row(d),
        out_shape=jax.ShapeDtypeStruct((t, d), F32),
        scratch_shapes=[pltpu.VMEM((2, 2 * tm, d), F32), pltpu.SemaphoreType.DMA((2,))],
        compiler_params=_cparams("arbitrary"),
        name="moe_combine",
    )(dest3, dest3, x1, wt, y)


def _moe(x1, h, idx, wt, w13, w2, layer):
    t, d = h.shape
    top_i = idx[:, :TOP_K]
    flat_e = top_i.reshape(-1)
    onehot = (flat_e[:, None] == jnp.arange(N_EXPERTS, dtype=I32)[None, :]).astype(I32)
    csum = jnp.cumsum(onehot, axis=0)
    rank = jnp.take_along_axis(csum, flat_e[:, None], axis=1)[:, 0] - 1
    counts = csum[-1]
    padded = (counts + MOE_ROWS - 1) // MOE_ROWS * MOE_ROWS
    pad_end = jnp.cumsum(padded)
    pad_start = pad_end - padded
    dest = pad_start[flat_e] + rank
    n_blocks = -(-(t * TOP_K + N_EXPERTS * (MOE_ROWS - 1)) // MOE_ROWS)
    n_rows = n_blocks * MOE_ROWS
    flat_tok = jnp.arange(t * TOP_K, dtype=I32) // TOP_K
    row_tok = jnp.zeros((n_rows,), I32).at[dest].set(flat_tok)
    blk_e = jnp.minimum(jnp.searchsorted(pad_end, jnp.arange(n_blocks, dtype=I32) * MOE_ROWS, side="right"),
                        N_EXPERTS - 1).astype(I32)
    y = _moe_experts(h, row_tok, blk_e, w13, w2, layer)
    return _moe_combine(x1, wt, y, dest.reshape(t, TOP_K).astype(I32))


def _rope_tables(pos):
    half = ROT_DIM // 2
    inv_freq = ROPE_THETA ** (-jnp.arange(half, dtype=F32) / half)
    ang = pos.astype(F32)[:, None] * inv_freq[None, :]
    cos, sin = jnp.cos(ang), jnp.sin(ang)
    n = pos.shape[0]
    ones = jnp.ones((n, HEAD_DIM - ROT_DIM), F32)
    zeros8 = jnp.zeros((n, half), F32)
    zeros = jnp.zeros((n, HEAD_DIM - ROT_DIM), F32)
    c = jnp.concatenate([cos, cos, ones], axis=1)
    s1 = jnp.concatenate([-sin, zeros8, zeros], axis=1)
    s2 = jnp.concatenate([zeros8, sin, zeros], axis=1)
    return tuple(jnp.tile(a, (1, LANES // HEAD_DIM)) for a in (c, s1, s2))


def _layer_weights(l, p):
    d = p["w_in"].shape[1]
    w_in = p["w_in"][l]
    sizes = (A_W, A_W, A_W, B_QK_W, B_QK_W, B_V_W, C_QK_W, C_QK_W, C_V_W, C_GATE_RANK, C_V_W, M_W, N_BRANCH * d)
    offs = np.concatenate([[0], np.cumsum(sizes)])
    seg = lambda j: w_in[:, int(offs[j]):int(offs[j + 1])]
    ca_pad = jnp.pad(seg(9), ((0, 0), (0, LANES - C_GATE_RANK)))
    w_all = jnp.concatenate([seg(j) for j in (0, 1, 2, 3, 4, 5, 6, 7, 8, 10, 11, 12)] + [ca_pad], axis=1)
    blk = np.kron(np.eye(A_W // HEAD_DIM, dtype=np.float32), np.full((HEAD_DIM, HEAD_DIM), 1.0 / HEAD_DIM, np.float32))
    tile = lambda g, n: jnp.tile(g, n)[None, :].astype(F32)
    w_br = p["w_branch"][l].astype(BF16)
    o1, o2, o3 = A_W, A_W + B_V_W, A_W + B_V_W + C_V_W
    lw = {
        "norm1_g": p["norm1_g"][l][None, :], "w_in": w_all.astype(BF16), "bd": jnp.asarray(blk, BF16),
        "a_qn": tile(p["a_q_norm"][l], A_HEADS), "a_kn": tile(p["a_k_norm"][l], A_HEADS),
        "b_qn": tile(p["b_q_norm"][l], 2 * B_HEADS), "b_kn": tile(p["b_k_norm"][l], 2 * B_HEADS),
        "m_qn": tile(p["m_q_norm"][l], M_HEADS), "m_kn": tile(p["m_k_norm"][l], M_HEADS),
        "c_wa": jnp.pad(p["c_w_alpha"][l], ((0, LANES - C_GATE_RANK), (0, 0))).astype(BF16),
        "c_ba": p["c_b_alpha"][l][None, :],
        "mem_norm": p["mem_norm"][l][None, :], "w_mem_kv": p["w_mem_kv"][l].astype(BF16),
        "w_a": w_br[:o1], "w_b": w_br[o1:o2], "w_c": w_br[o2:o3], "w_m": w_br[o3:],
        "w_o": p["w_out"][l].astype(BF16), "norm2_g": p["norm2_g"][l][None, :],
        "b_lambda": p["b_lambda"][l], "b_subln": p["b_subln"][l][None, :], "c_on": p["c_out_norm"][l][None, :],
        "a_rel_bias": p["a_rel_bias"][l],
    }
    if l % 2 == 0:
        lw["ffn_w13"] = p["ffn_w13"][l // 2].astype(BF16)
        lw["ffn_w2"] = p["ffn_w2"][l // 2].astype(BF16)
    else:
        r = jnp.pad(p["moe_router"][l // 2], ((0, 0), (0, LANES - N_EXPERTS)))
        r_hi = r.astype(BF16)
        lw["router_hi"] = r_hi
        lw["router_lo"] = (r - r_hi.astype(F32)).astype(BF16)
        lw["moe_w13"] = p["moe_w13_bf16"]
        lw["moe_w2"] = p["moe_w2_bf16"]
    return lw


def _channel_mixer(l, x, br, lw):
    if l % 2 == 0:
        return _merge_ffn(x, br, lw)
    x1, h, idx, wt = _merge_router(x, br, lw)
    return _moe(x1, h, idx, wt, lw["moe_w13"], lw["moe_w2"], l // 2)


def kernel(x_prompt, x_sample, cache_a_k, cache_a_v, cache_b_k, cache_b_v, state_c, cache_mem_k, cache_mem_v,
           mem_prompt, norm1_g, w_in, a_q_norm, a_k_norm, a_rel_bias, b_q_norm, b_k_norm, b_lambda, b_subln,
           c_w_alpha, c_b_alpha, c_out_norm, mem_norm, w_mem_kv, m_q_norm, m_k_norm, w_branch, w_out, norm2_g,
           ffn_w13, ffn_w2, moe_router, moe_w13, moe_w2):
    p = dict(norm1_g=norm1_g, w_in=w_in, a_q_norm=a_q_norm, a_k_norm=a_k_norm, a_rel_bias=a_rel_bias,
             b_q_norm=b_q_norm, b_k_norm=b_k_norm, b_lambda=b_lambda, b_subln=b_subln, c_w_alpha=c_w_alpha,
             c_b_alpha=c_b_alpha, c_out_norm=c_out_norm, mem_norm=mem_norm, w_mem_kv=w_mem_kv, m_q_norm=m_q_norm,
             m_k_norm=m_k_norm, w_branch=w_branch, w_out=w_out, norm2_g=norm2_g, ffn_w13=ffn_w13, ffn_w2=ffn_w2,
             moe_router=moe_router, moe_w13_bf16=moe_w13.astype(BF16), moe_w2_bf16=moe_w2.astype(BF16))
    depth = w_in.shape[0]
    nb, seq, d = x_prompt.shape
    sb, t_new, _ = x_sample.shape
    past = cache_b_k.shape[2]
    n_mem = mem_prompt.shape[1]
    a_keep = min(A_PAST, seq)
    assert t_new == CHUNK and cache_a_k.shape[2] == A_PAST and seq % TOKEN_TILE == 0 and (sb * t_new) % TOKEN_TILE == 0

    rope_p = _rope_tables(jnp.tile(jnp.arange(seq), nb))
    rope_s = _rope_tables(jnp.tile(past + jnp.arange(t_new), sb))
    xp = x_prompt.reshape(nb * seq, d)
    xs = x_sample.reshape(sb * t_new, d)
    c_tile = min(C_TILE, seq)
    cache_bk = cache_b_k.reshape(depth, sb, past, B_QK_W)
    cache_bv = cache_b_v.reshape(depth, sb, past, B_V_W)
    outs = {k: [] for k in ("a_kp", "a_vp", "a_ks", "a_vs", "b_kp", "b_vp", "b_ks", "b_vs", "c_sp", "c_ss",
                            "m_kp", "m_vp")}

    for l in range(depth):
        lam_init = 0.8 - 0.6 * math.exp(-0.3 * l)
        lw = _layer_weights(l, p)

        pr = _inproj(xp, lw, rope_p)
        r3 = lambda a, b_=nb, s_=seq: a.reshape(b_, s_, a.shape[-1])
        br = {"gt": pr["gt"]}
        br["oa"] = _band_prompt(r3(pr["aq"]), r3(pr["ak"]), r3(pr["av"]), lw["a_rel_bias"]).reshape(nb * seq, A_W)
        br["ob"] = _diff_prompt(r3(pr["bq"]), r3(pr["bkh"]), r3(pr["bvh"]), lw["b_lambda"], lw["b_subln"],
                                lam_init).reshape(nb * seq, B_V_W)
        oc, c_fin = _gla(r3(pr["cq"]), r3(pr["ck"]), r3(pr["cv"]), r3(pr["cg"]), r3(pr["cr"]),
                         jnp.zeros((nb, C_QK_W, C_DV), F32), lw["c_on"], c_tile)
        br["oc"] = oc.reshape(nb * seq, C_V_W)
        mk, mv = _mem_kv(mem_prompt, lw)
        br["om"] = _dense_attention(r3(pr["mq"]), mk, mv, None, min(M_Q_TILE, seq)).reshape(nb * seq, M_W)
        outs["a_kp"].append(r3(pr["ak"])[:, seq - a_keep:].reshape(nb, a_keep, A_HEADS, HEAD_DIM))
        outs["a_vp"].append(r3(pr["av"])[:, seq - a_keep:].reshape(nb, a_keep, A_HEADS, HEAD_DIM))
        outs["b_kp"].append(pr["bk"].reshape(nb, seq, B_HEADS, 2, HEAD_DIM))
        outs["b_vp"].append(pr["bv"].reshape(nb, seq, B_HEADS, B_VDIM))
        outs["c_sp"].append(c_fin.reshape(nb, C_HEADS, C_DK, C_DV))
        outs["m_kp"].append(mk.reshape(nb, n_mem, M_HEADS, HEAD_DIM))
        outs["m_vp"].append(mv.reshape(nb, n_mem, M_HEADS, HEAD_DIM))
        xp = _channel_mixer(l, xp, br, lw)

        sr = _inproj(xs, lw, rope_s)
        s3 = lambda a: a.reshape(sb, t_new, a.shape[-1])
        bs = {"gt": sr["gt"]}
        ka = jnp.concatenate([cache_a_k[l].reshape(sb, A_PAST, A_W), s3(sr["ak"])], axis=1)
        va = jnp.concatenate([cache_a_v[l].reshape(sb, A_PAST, A_W), s3(sr["av"])], axis=1)
        bias_s = _rel_bias(lw["a_rel_bias"], t_new, A_PAST + t_new, A_PAST)
        bs["oa"] = _dense_attention(s3(sr["aq"]), ka, va, bias_s, t_new).reshape(sb * t_new, A_W)
        bs["ob"] = _diff_sample(s3(sr["bq"]), cache_bk, cache_bv, l, s3(sr["bkh"]), s3(sr["bvh"]),
                                lw["b_lambda"], lw["b_subln"], lam_init).reshape(sb * t_new, B_V_W)
        oc, c_new = _gla(s3(sr["cq"]), s3(sr["ck"]), s3(sr["cv"]), s3(sr["cg"]), s3(sr["cr"]),
                         state_c[l].reshape(sb, C_QK_W, C_DV), lw["c_on"], t_new)
        bs["oc"] = oc.reshape(sb * t_new, C_V_W)
        bs["om"] = _dense_attention(s3(sr["mq"]), cache_mem_k[l].reshape(sb, n_mem, M_W),
                                    cache_mem_v[l].reshape(sb, n_mem, M_W), None, t_new).reshape(sb * t_new, M_W)
        outs["a_ks"].append(sr["ak"].reshape(sb, t_new, A_HEADS, HEAD_DIM))
        outs["a_vs"].append(sr["av"].reshape(sb, t_new, A_HEADS, HEAD_DIM))
        outs["b_ks"].append(sr["bk"].reshape(sb, t_new, B_HEADS, 2, HEAD_DIM))
        outs["b_vs"].append(sr["bv"].reshape(sb, t_new, B_HEADS, B_VDIM))
        outs["c_ss"].append(c_new.reshape(sb, C_HEADS, C_DK, C_DV))
        xs = _channel_mixer(l, xs, bs, lw)

    st = {k: jnp.stack(v) for k, v in outs.items()}
    return (xp.reshape(nb, seq, d), xs.reshape(sb, t_new, d),
            st["a_kp"], st["a_vp"], st["a_ks"], st["a_vs"], st["b_kp"], st["b_vp"], st["b_ks"], st["b_vs"],
            st["c_sp"], st["c_ss"], st["m_kp"], st["m_vp"])
```

```python
import functools
import math

import numpy as np
import jax
import jax.numpy as jnp
from jax import lax
from jax.experimental import pallas as pl
from jax.experimental.pallas import tpu as pltpu

F32 = jnp.float32
BF16 = jnp.bfloat16
I32 = jnp.int32

CHUNK = 64
HEAD_DIM = 64
A_HEADS = 8
A_BAND_CHUNKS = 8
A_PAST = A_BAND_CHUNKS * CHUNK
REL_CLIP = 128
B_HEADS = 4
B_VDIM = 2 * HEAD_DIM
ROT_DIM = HEAD_DIM // 4
ROPE_THETA = 500000.0
C_HEADS = 4
C_DK = 64
C_DV = 128
C_GATE_RANK = 16
C_TAU = 16.0
M_HEADS = 4
N_BRANCH = 4
N_EXPERTS = 8
TOP_K = 2
EPS = 1e-6
NEG_INF = -1e30

A_W = A_HEADS * HEAD_DIM
B_QK_W = B_HEADS * 2 * HEAD_DIM
B_V_W = B_HEADS * B_VDIM
C_QK_W = C_HEADS * C_DK
C_V_W = C_HEADS * C_DV
M_W = M_HEADS * HEAD_DIM

LANES = 128
MXU_TILE = 256
VMEM_LIMIT_BYTES = 56 * 1024 * 1024

TOKEN_TILE = 256
A_Q_TILE = 256
B_Q_TILE = 1024
B_K_TILE = 1024
B_HEADS_PER_STEP = 2
B_CACHE_TILE = 2048
C_TILE = 256
M_Q_TILE = 512
FF_CHUNK = 256
MOE_ROWS = 256
MOE_FF_CHUNK = 512
QK_SCALE = HEAD_DIM ** -0.5
LOG2_E = math.log2(math.e)

_SEG = {}
_off = 0
for _name, _w in (("aq", A_W), ("ak", A_W), ("av", A_W), ("bq", B_QK_W), ("bk", B_QK_W), ("bv", B_V_W),
                  ("cq", C_QK_W), ("ck", C_QK_W), ("cv", C_V_W), ("cr", C_V_W), ("mq", M_W)):
    _SEG[_name] = (_off, _w)
    _off += _w
GATE_OFF = _off


def _cparams(*sem):
    return pltpu.CompilerParams(dimension_semantics=sem, vmem_limit_bytes=VMEM_LIMIT_BYTES)


def _const_spec(shape):
    nd = len(shape)
    return pl.BlockSpec(shape, lambda *_: (0,) * nd, pipeline_mode=pl.Buffered(1))


def _dot(a, b):
    return jnp.dot(a, b, preferred_element_type=F32)


def _dot_nt(a, b):
    return lax.dot_general(a, b, (((1,), (1,)), ((), ())), preferred_element_type=F32)


def _dot_tn(a, b):
    return lax.dot_general(a, b, (((0,), (0,)), ((), ())), preferred_element_type=F32)


def _rms(x, gain):
    return x * lax.rsqrt(jnp.mean(x * x, axis=-1, keepdims=True) + EPS) * gain


def _head_norm(y, bd_ref, gain):
    sq = (y * y).astype(BF16)
    ms = jnp.concatenate([_dot(sq[:, c:c + MXU_TILE], bd_ref[:MXU_TILE, :MXU_TILE])
                          for c in range(0, y.shape[-1], MXU_TILE)], axis=1)
    return y * lax.rsqrt(ms + EPS) * gain


def _sigmoid(x):
    return 1.0 / (1.0 + jnp.exp(-x))


def _inproj_body(x_ref, g1_ref, w_ref, bd_ref, aqn_ref, akn_ref, bqn_ref, bkn_ref, mqn_ref, cwa_ref, cba_ref,
                 rc_ref, rs1_ref, rs2_ref,
                 aq_o, ak_o, av_o, bq_o, bk_o, bkh_o, bv_o, bvh_o, cq_o, ck_o, cv_o, cg_o, cr_o, mq_o, gt_o):
    x = x_ref[...]
    h = _rms(x, g1_ref[...]).astype(BF16)
    d_model = x.shape[-1]

    def proj(name):
        lo, w = _SEG[name]
        return _dot(h, w_ref[:, lo:lo + w])

    def rope(y):
        reps = y.shape[-1] // LANES
        c = jnp.concatenate([rc_ref[...]] * reps, axis=1)
        s1 = jnp.concatenate([rs1_ref[...]] * reps, axis=1)
        s2 = jnp.concatenate([rs2_ref[...]] * reps, axis=1)
        half = ROT_DIM // 2
        return y * c + pltpu.roll(y, y.shape[-1] - half, 1) * s1 + pltpu.roll(y, half, 1) * s2

    aq_o[...] = (_head_norm(proj("aq"), bd_ref, aqn_ref[...]) * QK_SCALE).astype(BF16)
    ak_o[...] = _head_norm(proj("ak"), bd_ref, akn_ref[...])
    av_o[...] = proj("av")
    bq_o[...] = (rope(_head_norm(proj("bq"), bd_ref, bqn_ref[...])) * (QK_SCALE * LOG2_E)).astype(BF16)
    bk = rope(_head_norm(proj("bk"), bd_ref, bkn_ref[...]))
    bk_o[...] = bk
    bkh_o[...] = bk.astype(BF16)
    bv = proj("bv")
    bv_o[...] = bv
    bvh_o[...] = bv.astype(BF16)
    cq_o[...] = proj("cq") * (C_DK ** -0.5)
    ck_o[...] = proj("ck")
    cv_o[...] = proj("cv").astype(BF16)
    cr_o[...] = proj("cr").astype(BF16)
    mq_o[...] = (_head_norm(proj("mq"), bd_ref, mqn_ref[...]) * QK_SCALE).astype(BF16)
    ca = _dot(h, w_ref[:, GATE_OFF + N_BRANCH * d_model:])
    z = _dot(ca.astype(BF16), cwa_ref[...]) + cba_ref[...]
    cg_o[...] = (jnp.minimum(z, 0.0) - jnp.log1p(jnp.exp(-jnp.abs(z)))) * (1.0 / C_TAU)
    for c in range(N_BRANCH):
        lo = GATE_OFF + c * d_model
        gt_o[:, c * d_model:(c + 1) * d_model] = _sigmoid(_dot(h, w_ref[:, lo:lo + d_model])).astype(BF16)


def _inproj(x, lw, rope_tabs):
    t, d = x.shape
    tm = TOKEN_TILE
    wcols = lw["w_in"].shape[1]
    row = lambda w: pl.BlockSpec((tm, w), lambda i: (i, 0))
    outs = [("aq", A_W, BF16), ("ak", A_W, F32), ("av", A_W, F32), ("bq", B_QK_W, BF16), ("bk", B_QK_W, F32),
            ("bkh", B_QK_W, BF16), ("bv", B_V_W, F32), ("bvh", B_V_W, BF16), ("cq", C_QK_W, F32),
            ("ck", C_QK_W, F32), ("cv", C_V_W, BF16), ("cg", C_QK_W, F32), ("cr", C_V_W, BF16),
            ("mq", M_W, BF16), ("gt", N_BRANCH * d, BF16)]
    res = pl.pallas_call(
        _inproj_body,
        grid=(t // tm,),
        in_specs=[row(d), _const_spec((1, d)), _const_spec((d, wcols)), _const_spec((A_W, A_W)),
                  _const_spec((1, A_W)), _const_spec((1, A_W)), _const_spec((1, B_QK_W)), _const_spec((1, B_QK_W)),
                  _const_spec((1, M_W)), _const_spec((LANES, C_QK_W)), _const_spec((1, C_QK_W)),
                  row(LANES), row(LANES), row(LANES)],
        out_specs=[row(w) for _, w, _ in outs],
        out_shape=[jax.ShapeDtypeStruct((t, w), dt) for _, w, dt in outs],
        compiler_params=_cparams("parallel"),
        name="inproj",
    )(x, lw["norm1_g"], lw["w_in"], lw["bd"], lw["a_qn"], lw["a_kn"], lw["b_qn"], lw["b_kn"], lw["m_qn"],
      lw["c_wa"], lw["c_ba"], *rope_tabs)
    return {n: r for (n, _, _), r in zip(outs, res)}


def _pair_attention(q2, k2, v2, bias_even, bias_odd):
    tq = q2.shape[0]
    lane = lax.broadcasted_iota(I32, q2.shape, 1)
    zero = jnp.zeros_like(q2)
    qq = jnp.concatenate([jnp.where(lane < HEAD_DIM, q2, zero), jnp.where(lane >= HEAD_DIM, q2, zero)], axis=0)
    s = _dot_nt(qq, k2)
    if bias_even is not None:
        s = s + jnp.concatenate([bias_even, bias_odd], axis=0)
    m = jnp.max(s, axis=-1, keepdims=True)
    p = jnp.exp(s - m)
    l = jnp.sum(p, axis=-1, keepdims=True)
    o = _dot(p.astype(BF16), v2) / l
    return jnp.where(lane < HEAD_DIM, o[:tq], o[tq:])


def _band_body(q_ref, k0_ref, k1_ref, k2_ref, v0_ref, v1_ref, v2_ref, bias_ref, o_ref):
    i = pl.program_id(1)
    tq = q_ref.shape[1]
    k = jnp.concatenate([k0_ref[0], k1_ref[0], k2_ref[0]], axis=0).astype(BF16)
    v = jnp.concatenate([v0_ref[0], v1_ref[0], v2_ref[0]], axis=0).astype(BF16)
    col = lax.broadcasted_iota(I32, (tq, 3 * tq), 1)
    pad = jnp.where(col >= 2 * tq - i * tq, 0.0, NEG_INF)
    for p in range(A_HEADS // 2):
        sl = slice(p * LANES, (p + 1) * LANES)
        o_ref[0, :, sl] = _pair_attention(q_ref[0, :, sl], k[:, sl], v[:, sl],
                                          bias_ref[2 * p] + pad, bias_ref[2 * p + 1] + pad).astype(o_ref.dtype)


def _rel_bias(table, n_q, n_k, offset):
    period = n_q + n_k
    k = np.arange(period)
    dist = np.where(k < n_k, offset - k, offset - k + period)
    w = table[:, np.clip(dist, -REL_CLIP, REL_CLIP) + REL_CLIP].astype(F32)
    heads = table.shape[0]
    return jnp.tile(w, (1, n_q))[:, :n_q * (period - 1)].reshape(heads, n_q, period - 1)[:, :, :n_k]


def _band_bias(table, tq):
    r = np.arange(tq)[:, None]
    c = np.arange(3 * tq)[None, :]
    qc = (r + 2 * tq) // CHUNK
    kc = c // CHUNK
    vis = (kc <= qc) & (kc >= qc - A_BAND_CHUNKS)
    return jnp.where(jnp.asarray(vis)[None], _rel_bias(table, tq, 3 * tq, 2 * tq), NEG_INF)


def _band_prompt(aq, ak, av, table):
    b, s, w = aq.shape
    tq = A_Q_TILE
    assert A_PAST == 2 * tq and s % tq == 0
    qspec = pl.BlockSpec((1, tq, w), lambda bi, i: (bi, i, 0))
    kspec = lambda back: pl.BlockSpec((1, tq, w), lambda bi, i: (bi, jnp.maximum(i - back, 0), 0))
    return pl.pallas_call(
        _band_body,
        grid=(b, s // tq),
        in_specs=[qspec, kspec(2), kspec(1), kspec(0), kspec(2), kspec(1), kspec(0),
                  _const_spec((A_HEADS, tq, 3 * tq))],
        out_specs=qspec,
        out_shape=jax.ShapeDtypeStruct((b, s, w), BF16),
        compiler_params=_cparams("parallel", "parallel"),
        name="band_attention",
    )(aq, ak, ak, ak, av, av, av, _band_bias(table, tq))


def _dense_attn_body(*refs, heads, has_bias):
    if has_bias:
        q_ref, k_ref, v_ref, bias_ref, o_ref = refs
    else:
        q_ref, k_ref, v_ref, o_ref = refs
    k = k_ref[0].astype(BF16)
    v = v_ref[0].astype(BF16)
    for p in range(heads // 2):
        sl = slice(p * LANES, (p + 1) * LANES)
        be, bo = (bias_ref[2 * p], bias_ref[2 * p + 1]) if has_bias else (None, None)
        o_ref[0, :, sl] = _pair_attention(q_ref[0, :, sl], k[:, sl], v[:, sl], be, bo).astype(o_ref.dtype)


def _dense_attention(q, k, v, bias, tq):
    b, sq, w = q.shape
    sk = k.shape[1]
    heads = w // HEAD_DIM
    qspec = pl.BlockSpec((1, tq, w), lambda bi, i: (bi, i, 0))
    kspec = pl.BlockSpec((1, sk, w), lambda bi, i: (bi, 0, 0))
    in_specs = [qspec, kspec, kspec]
    args = [q, k, v]
    if bias is not None:
        in_specs.append(_const_spec(bias.shape))
        args.append(bias)
    return pl.pallas_call(
        functools.partial(_dense_attn_body, heads=heads, has_bias=bias is not None),
        grid=(b, sq // tq),
        in_specs=in_specs,
        out_specs=qspec,
        out_shape=jax.ShapeDtypeStruct((b, sq, w), BF16),
        compiler_params=_cparams("parallel", "parallel"),
        name="dense_attention",
    )(*args)


def _memkv_body(mem_ref, g_ref, w_ref, bd_ref, kn_ref, k_o, v_o):
    h = _rms(mem_ref[0], g_ref[...]).astype(BF16)
    kv = _dot(h, w_ref[...])
    k_o[0] = _head_norm(kv[:, :M_W], bd_ref, kn_ref[...])
    v_o[0] = kv[:, M_W:]


def _mem_kv(mem, lw):
    b, n, d = mem.shape
    spec = pl.BlockSpec((1, n, M_W), lambda bi: (bi, 0, 0))
    return pl.pallas_call(
        _memkv_body,
        grid=(b,),
        in_specs=[pl.BlockSpec((1, n, d), lambda bi: (bi, 0, 0)), _const_spec((1, d)), _const_spec((d, 2 * M_W)),
                  _const_spec((A_W, A_W)), _const_spec((1, M_W))],
        out_specs=[spec, spec],
        out_shape=[jax.ShapeDtypeStruct((b, n, M_W), F32)] * 2,
        compiler_params=_cparams("parallel"),
        name="mem_kv",
    )(mem, lw["mem_norm"], lw["w_mem_kv"], lw["bd"], lw["m_kn"])


def _diff_init(q, qq_scr, m_scr, l_scr, acc_scr):
    lane = lax.broadcasted_iota(I32, q.shape, 1)
    zero = jnp.zeros_like(q)
    tq = q.shape[0]
    qq_scr[:tq] = jnp.where(lane < HEAD_DIM, q, zero)
    qq_scr[tq:] = jnp.where(lane >= HEAD_DIM, q, zero)
    m_scr[...] = jnp.full(m_scr.shape, NEG_INF, F32)
    l_scr[...] = jnp.zeros(l_scr.shape, F32)
    acc_scr[...] = jnp.zeros(acc_scr.shape, F32)


def _diff_step(k, v, qq_scr, m_scr, l_scr, acc_scr, visible):
    s = _dot_nt(qq_scr[...], k)
    if visible is not None:
        s = jnp.where(visible, s, NEG_INF)
    m_prev = m_scr[...]
    m_new = jnp.maximum(m_prev, jnp.max(s, axis=-1, keepdims=True))
    alpha = jnp.exp2(m_prev - m_new)
    tk = s.shape[-1]
    m_wide = m_new[:, :tk] if tk <= LANES else jnp.concatenate([m_new] * (tk // LANES), axis=1)
    p = jnp.exp2(s - m_wide)
    l_scr[...] = alpha * l_scr[...] + jnp.sum(p, axis=-1, keepdims=True)
    acc_scr[...] = alpha * acc_scr[...] + _dot(p.astype(BF16), v)
    m_scr[...] = m_new


def _diff_finish(lam_ref, sub_ref, l_scr, acc_scr, lam_init):
    tq = acc_scr.shape[0] // 2
    lam_p = lam_ref[...]
    lam = (jnp.exp(jnp.sum(lam_p[0:1] * lam_p[1:2], axis=-1, keepdims=True))
           - jnp.exp(jnp.sum(lam_p[2:3] * lam_p[3:4], axis=-1, keepdims=True)) + lam_init)
    o = acc_scr[:tq] / l_scr[:tq] - lam * (acc_scr[tq:] / l_scr[tq:])
    return _rms(o, sub_ref[...]) * (1.0 - lam_init)


def _diff_scratch(tq):
    g = B_HEADS_PER_STEP
    return [pltpu.VMEM((g, 2 * tq, LANES), BF16), pltpu.VMEM((g, 2 * tq, LANES), F32),
            pltpu.VMEM((g, 2 * tq, LANES), F32), pltpu.VMEM((g, 2 * tq, B_VDIM), F32)]


def _head_lanes(g):
    return slice(g * LANES, (g + 1) * LANES)


def _diff_prompt_body(qi_ref, ki_ref, q_ref, k_ref, v_ref, lam_ref, sub_ref, o_ref,
                      qq_scr, m_scr, l_scr, acc_scr, *, lam_init):
    s_id = pl.program_id(2)
    qi = qi_ref[s_id]
    ki = ki_ref[s_id]
    tq = q_ref.shape[1]
    tk = k_ref.shape[1]
    heads = range(B_HEADS_PER_STEP)
    state = lambda g: (qq_scr.at[g], m_scr.at[g], l_scr.at[g], acc_scr.at[g])

    @pl.when(ki == 0)
    def _():
        for g in heads:
            _diff_init(q_ref[0, :, _head_lanes(g)], *state(g))

    needs_mask = (ki + 1) * tk > qi * tq

    @pl.when(needs_mask)
    def _():
        row = lax.broadcasted_iota(I32, (2 * tq, tk), 0)
        row = jnp.where(row >= tq, row - tq, row)
        col = lax.broadcasted_iota(I32, (2 * tq, tk), 1)
        chunk_shift = int(math.log2(CHUNK))
        visible = ((ki * tk + col) >> chunk_shift) <= ((qi * tq + row) >> chunk_shift)
        for g in heads:
            _diff_step(k_ref[0, :, _head_lanes(g)], v_ref[0, :, _head_lanes(g)], *state(g), visible)

    @pl.when(jnp.logical_not(needs_mask))
    def _():
        for g in heads:
            _diff_step(k_ref[0, :, _head_lanes(g)], v_ref[0, :, _head_lanes(g)], *state(g), None)

    last_ki = ((qi + 1) * tq - 1) // tk

    @pl.when(ki == last_ki)
    def _():
        for g in heads:
            o_ref[0, :, _head_lanes(g)] = _diff_finish(lam_ref, sub_ref, l_scr.at[g], acc_scr.at[g],
                                                       lam_init).astype(o_ref.dtype)


def _diff_prompt(bq, bkh, bvh, lam_p, sub, lam_init):
    b, s, w = bq.shape
    tq = min(B_Q_TILE, s)
    tk = min(B_K_TILE, s)
    gw = B_HEADS_PER_STEP * LANES
    steps = [(qi, ki) for qi in range(s // tq) for ki in range(((qi + 1) * tq - 1) // tk + 1)]
    qi_tab = jnp.asarray([p[0] for p in steps], I32)
    ki_tab = jnp.asarray([p[1] for p in steps], I32)
    qspec = pl.BlockSpec((1, tq, gw), lambda bi, h, st, qt, kt: (bi, qt[st], h))
    kspec = pl.BlockSpec((1, tk, gw), lambda bi, h, st, qt, kt: (bi, kt[st], h))
    cspec = lambda shp: pl.BlockSpec(shp, lambda bi, h, st, qt, kt: (0, 0))
    return pl.pallas_call(
        functools.partial(_diff_prompt_body, lam_init=lam_init),
        grid_spec=pltpu.PrefetchScalarGridSpec(
            num_scalar_prefetch=2,
            grid=(b, B_HEADS // B_HEADS_PER_STEP, len(steps)),
            in_specs=[qspec, kspec, kspec, cspec((4, HEAD_DIM)), cspec((1, B_VDIM))],
            out_specs=qspec,
            scratch_shapes=_diff_scratch(tq)),
        out_shape=jax.ShapeDtypeStruct((b, s, w), BF16),
        compiler_params=_cparams("parallel", "parallel", "arbitrary"),
        name="diff_attention",
    )(qi_tab, ki_tab, bq, bkh, bvh, lam_p, sub)


def _diff_sample_body(q_ref, ck_ref, cv_ref, nk_ref, nv_ref, lam_ref, sub_ref, o_ref,
                      qq_scr, m_scr, l_scr, acc_scr, *, lam_init):
    t = pl.program_id(2)
    heads = range(B_HEADS_PER_STEP)
    state = lambda g: (qq_scr.at[g], m_scr.at[g], l_scr.at[g], acc_scr.at[g])

    @pl.when(t == 0)
    def _():
        for g in heads:
            _diff_init(q_ref[0, :, _head_lanes(g)], *state(g))

    for g in heads:
        _diff_step(ck_ref[0, :, _head_lanes(g)].astype(BF16), cv_ref[0, :, _head_lanes(g)].astype(BF16),
                   *state(g), None)

    @pl.when(t == pl.num_programs(2) - 1)
    def _():
        for g in heads:
            _diff_step(nk_ref[0, :, _head_lanes(g)], nv_ref[0, :, _head_lanes(g)], *state(g), None)
            o_ref[0, :, _head_lanes(g)] = _diff_finish(lam_ref, sub_ref, l_scr.at[g], acc_scr.at[g],
                                                       lam_init).astype(o_ref.dtype)


def _diff_sample(bq, cache_k, cache_v, layer, bkh, bvh, lam_p, sub, lam_init):
    b, tq, w = bq.shape
    past = cache_k.shape[2]
    tc = min(B_CACHE_TILE, past)
    gw = B_HEADS_PER_STEP * LANES
    assert past % tc == 0 and tq == CHUNK and past % CHUNK == 0
    qspec = pl.BlockSpec((1, tq, gw), lambda bi, h, t: (bi, 0, h))
    cspec = pl.BlockSpec((None, 1, tc, gw), lambda bi, h, t: (layer, bi, t, h))
    pspec = lambda shp: pl.BlockSpec(shp, lambda bi, h, t: (0, 0))
    return pl.pallas_call(
        functools.partial(_diff_sample_body, lam_init=lam_init),
        grid=(b, B_HEADS // B_HEADS_PER_STEP, past // tc),
        in_specs=[qspec, cspec, cspec, qspec, qspec, pspec((4, HEAD_DIM)), pspec((1, B_VDIM))],
        out_specs=qspec,
        scratch_shapes=_diff_scratch(tq),
        out_shape=jax.ShapeDtypeStruct((b, tq, w), BF16),
        compiler_params=_cparams("parallel", "parallel", "arbitrary"),
        name="diff_attention_sample",
    )(bq, cache_k, cache_v, bkh, bvh, lam_p, sub)


def _gla_decay_matrix(t):
    levels = int(math.log2(t))
    d = np.zeros((levels + 2, t, t), np.float32)
    u = np.arange(t)[None, :]
    i = np.arange(t)[:, None]
    for lv in range(levels):
        g = t >> lv
        base = i - i % g
        r = base + g // 2 - 1
        second = (i % g) >= g // 2
        d[lv] = np.where(second, (u > r) & (u <= i), (u > i) & (u <= r))
    d[levels] = u <= i
    d[levels + 1] = u > i
    return jnp.asarray(d.reshape((levels + 2) * t, t), BF16)


def _gla_body(q_ref, k_ref, v_ref, g_ref, r_ref, s0_ref, d_ref, on_ref, o_ref, sf_ref, s_scr):
    c = pl.program_id(1)
    t = q_ref.shape[1]
    levels = int(math.log2(t))

    @pl.when(c == 0)
    def _():
        s_scr[...] = s0_ref[0]

    q = q_ref[0]
    k = k_ref[0]
    g = g_ref[0]
    g_hi = g.astype(BF16)
    g_lo = (g - g_hi.astype(F32)).astype(BF16)
    g2 = jnp.concatenate([g_hi, g_lo], axis=1)

    def exponent(block):
        e = _dot(d_ref[block * t:(block + 1) * t, :], g2)
        return e[:, :C_QK_W] + e[:, C_QK_W:]

    lane = lax.broadcasted_iota(I32, (t, C_QK_W), 1)
    head_lanes = [(lane >= h * C_DK) & (lane < (h + 1) * C_DK) for h in range(C_HEADS)]
    tok = lax.broadcasted_iota(I32, (t, C_QK_W), 0)
    ri = lax.broadcasted_iota(I32, (t, t), 0)
    ci = lax.broadcasted_iota(I32, (t, t), 1)
    zero = jnp.zeros((t, C_QK_W), BF16)
    kb = k.astype(BF16)
    qb = q.astype(BF16)

    att = [jnp.where(ri == ci, _dot_nt(jnp.where(head_lanes[h], qb, zero), kb), 0.0) for h in range(C_HEADS)]
    for lv in range(levels):
        shift = levels - lv
        w = jnp.exp(exponent(lv))
        second = ((tok >> (shift - 1)) & 1) == 1
        ql = jnp.where(second, q * w, 0.0).astype(BF16)
        kl = jnp.where(second, 0.0, k * w).astype(BF16)
        same = (ri >> shift) == (ci >> shift)
        for h in range(C_HEADS):
            att[h] = att[h] + jnp.where(same, _dot_nt(jnp.where(head_lanes[h], ql, zero), kl), 0.0)

    state = s_scr[...]
    state_b = state.astype(BF16)
    q_in = (q * jnp.exp(exponent(levels))).astype(BF16)
    k_out = (k * jnp.exp(exponent(levels + 1))).astype(BF16)
    ones = jnp.ones((t, C_DV), BF16)
    decay = jnp.exp(_dot_tn(g_hi, ones) + _dot_tn(g_lo, ones))
    srow = lax.broadcasted_iota(I32, (C_QK_W, C_DV), 0)
    new_state = decay * state
    for h in range(C_HEADS):
        sl = slice(h * C_DV, (h + 1) * C_DV)
        v = v_ref[0, :, sl]
        o = _dot(att[h].astype(BF16), v) + _dot(jnp.where(head_lanes[h], q_in, zero), state_b)
        r = r_ref[0, :, sl].astype(F32)
        o_ref[0, :, sl] = (_rms(o, on_ref[...]) * (r * _sigmoid(r))).astype(o_ref.dtype)
        upd = _dot_tn(k_out, v)
        new_state = new_state + jnp.where((srow >= h * C_DK) & (srow < (h + 1) * C_DK), upd, 0.0)
    s_scr[...] = new_state

    @pl.when(c == pl.num_programs(1) - 1)
    def _():
        sf_ref[0] = new_state


def _gla(cq, ck, cv, cg, cr, state0, c_on, t):
    b, s, _ = cq.shape
    levels = int(math.log2(t))
    assert 1 << levels == t and s % t == 0
    tok = lambda w: pl.BlockSpec((1, t, w), lambda bi, c: (bi, c, 0))
    sspec = pl.BlockSpec((1, C_QK_W, C_DV), lambda bi, c: (bi, 0, 0))
    return pl.pallas_call(
        _gla_body,
        grid=(b, s // t),
        in_specs=[tok(C_QK_W), tok(C_QK_W), tok(C_V_W), tok(C_QK_W), tok(C_V_W), sspec,
                  _const_spec(((levels + 2) * t, t)), _const_spec((1, C_DV))],
        out_specs=[tok(C_V_W), sspec],
        out_shape=[jax.ShapeDtypeStruct((b, s, C_V_W), BF16), jax.ShapeDtypeStruct((b, C_QK_W, C_DV), F32)],
        scratch_shapes=[pltpu.VMEM((C_QK_W, C_DV), F32)],
        compiler_params=_cparams("parallel", "arbitrary"),
        name="gla",
    )(cq, ck, cv, cg, cr, state0, _gla_decay_matrix(t), c_on)


def _merge_residual(x_ref, oa_ref, ob_ref, oc_ref, om_ref, gt_ref, wa_ref, wb_ref, wc_ref, wm_ref, wo_ref):
    d = x_ref.shape[-1]
    gate = lambda c: gt_ref[:, c * d:(c + 1) * d].astype(F32)
    y = (gate(0) * _dot(oa_ref[...], wa_ref[...]) + gate(1) * _dot(ob_ref[...], wb_ref[...])
         + gate(2) * _dot(oc_ref[...], wc_ref[...]) + gate(3) * _dot(om_ref[...], wm_ref[...]))
    return x_ref[...] + _dot(y.astype(BF16), wo_ref[...])


def _merge_ffn_body(x_ref, oa_ref, ob_ref, oc_ref, om_ref, gt_ref, wa_ref, wb_ref, wc_ref, wm_ref, wo_ref,
                    g2_ref, w13_ref, w2_ref, o_ref):
    x1 = _merge_residual(x_ref, oa_ref, ob_ref, oc_ref, om_ref, gt_ref, wa_ref, wb_ref, wc_ref, wm_ref, wo_ref)
    h = _rms(x1, g2_ref[...]).astype(BF16)
    d_ff = w2_ref.shape[0]
    acc = x1
    for c in range(d_ff // FF_CHUNK):
        lo = c * FF_CHUNK
        a = _dot(h, w13_ref[:, lo:lo + FF_CHUNK])
        b = _dot(h, w13_ref[:, d_ff + lo:d_ff + lo + FF_CHUNK])
        acc = acc + _dot((a * _sigmoid(a) * b).astype(BF16), w2_ref[lo:lo + FF_CHUNK, :])
    o_ref[...] = acc


def _merge_router_body(x_ref, oa_ref, ob_ref, oc_ref, om_ref, gt_ref, wa_ref, wb_ref, wc_ref, wm_ref, wo_ref,
                       g2_ref, rh_ref, rl_ref, x1_o, h_o, idx_o, wt_o):
    x1 = _merge_residual(x_ref, oa_ref, ob_ref, oc_ref, om_ref, gt_ref, wa_ref, wb_ref, wc_ref, wm_ref, wo_ref)
    x1_o[...] = x1
    h = _rms(x1, g2_ref[...])
    h_o[...] = h
    h_hi = h.astype(BF16)
    h_lo = (h - h_hi.astype(F32)).astype(BF16)
    logits = _dot(h_hi, rh_ref[...]) + _dot(h_lo, rh_ref[...]) + _dot(h_hi, rl_ref[...])
    lane = lax.broadcasted_iota(I32, logits.shape, 1)
    lane_f = lane.astype(F32)
    neg = jnp.float32(-jnp.inf)
    lg = jnp.where(lane < N_EXPERTS, logits, neg)
    m1 = jnp.max(lg, axis=-1, keepdims=True)
    i1 = jnp.min(jnp.where(lg == m1, lane_f, float(LANES)), axis=-1, keepdims=True)
    lg2 = jnp.where(lane_f == i1, neg, lg)
    m2 = jnp.max(lg2, axis=-1, keepdims=True)
    i2 = jnp.min(jnp.where(lg2 == m2, lane_f, float(LANES)), axis=-1, keepdims=True)
    e = jnp.exp(m2 - m1)
    w1 = 1.0 / (1.0 + e)
    idx_o[...] = jnp.where(lane == 0, i1, jnp.where(lane == 1, i2, 0.0)).astype(I32)
    wt_o[...] = jnp.where(lane == 0, w1, jnp.where(lane == 1, e * w1, 0.0))


def _merge_specs(t, d):
    tm = TOKEN_TILE
    row = lambda w: pl.BlockSpec((tm, w), lambda i: (i, 0))
    specs = [row(d), row(A_W), row(B_V_W), row(C_V_W), row(M_W), row(N_BRANCH * d),
             _const_spec((A_W, d)), _const_spec((B_V_W, d)), _const_spec((C_V_W, d)), _const_spec((M_W, d)),
             _const_spec((d, d)), _const_spec((1, d))]
    return tm, row, specs


def _merge_args(x, br, lw):
    return (x, br["oa"], br["ob"], br["oc"], br["om"], br["gt"], lw["w_a"], lw["w_b"], lw["w_c"], lw["w_m"],
            lw["w_o"], lw["norm2_g"])


def _merge_ffn(x, br, lw):
    t, d = x.shape
    tm, row, specs = _merge_specs(t, d)
    d_ff = lw["ffn_w2"].shape[0]
    assert d_ff % FF_CHUNK == 0
    return pl.pallas_call(
        _merge_ffn_body,
        grid=(t // tm,),
        in_specs=specs + [_const_spec((d, 2 * d_ff)), _const_spec((d_ff, d))],
        out_specs=row(d),
        out_shape=jax.ShapeDtypeStruct((t, d), F32),
        compiler_params=_cparams("parallel"),
        name="merge_ffn",
    )(*_merge_args(x, br, lw), lw["ffn_w13"], lw["ffn_w2"])


def _merge_router(x, br, lw):
    t, d = x.shape
    tm, row, specs = _merge_specs(t, d)
    return pl.pallas_call(
        _merge_router_body,
        grid=(t // tm,),
        in_specs=specs + [_const_spec((d, LANES)), _const_spec((d, LANES))],
        out_specs=[row(d), row(d), row(LANES), row(LANES)],
        out_shape=[jax.ShapeDtypeStruct((t, d), F32), jax.ShapeDtypeStruct((t, d), F32),
                   jax.ShapeDtypeStruct((t, LANES), I32), jax.ShapeDtypeStruct((t, LANES), F32)],
        compiler_params=_cparams("parallel"),
        name="merge_router",
    )(*_merge_args(x, br, lw), lw["router_hi"], lw["router_lo"])


def _row_gather_copy(src_hbm, idx_ref, n, dst, sem):
    def issue(r, carry):
        pltpu.make_async_copy(src_hbm.at[pl.ds(idx_ref[0, 0, r], 1)], dst.at[pl.ds(r, 1)], sem).start()
        return carry
    lax.fori_loop(0, n, issue, 0, unroll=8)


def _row_gather_copy_unrolled(src_hbm, idx_ref, lo, hi, dst, sem):
    for r in range(lo, hi):
        pltpu.make_async_copy(src_hbm.at[pl.ds(idx_ref[0, 0, r], 1)], dst.at[pl.ds(r, 1)], sem).start()


def _row_gather_wait(dst, sem):
    pltpu.make_async_copy(dst, dst, sem).wait()


def _moe_body(blk_e_ref, cur_ref, nxt_ref, h_hbm, w13_ref, w2_ref, y_ref, xbuf, sem):
    i = pl.program_id(0)
    n = pl.num_programs(0)
    rows = xbuf.shape[1]
    slot = i % 2

    @pl.when(i == 0)
    def _():
        _row_gather_copy(h_hbm, cur_ref, rows, xbuf.at[0], sem.at[0])

    @pl.when(i + 1 < n)
    def _():
        _row_gather_copy_unrolled(h_hbm, nxt_ref, 0, rows, xbuf.at[1 - slot], sem.at[1 - slot])

    _row_gather_wait(xbuf.at[slot], sem.at[slot])
    x = xbuf[slot].astype(BF16)
    d_ff = w2_ref.shape[1]
    acc = jnp.zeros(y_ref.shape, F32)
    for c in range(d_ff // MOE_FF_CHUNK):
        lo = c * MOE_FF_CHUNK
        a = _dot(x, w13_ref[0, :, lo:lo + MOE_FF_CHUNK])
        b = _dot(x, w13_ref[0, :, d_ff + lo:d_ff + lo + MOE_FF_CHUNK])
        acc = acc + _dot((a * _sigmoid(a) * b).astype(BF16), w2_ref[0, lo:lo + MOE_FF_CHUNK, :])
    y_ref[...] = acc


def _moe_experts(h, row_tok, blk_e, w13, w2, layer):
    t, d = h.shape
    n_blocks = blk_e.shape[0]
    rows = MOE_ROWS
    d_ff = w2.shape[2]
    assert d_ff % MOE_FF_CHUNK == 0
    tok3 = row_tok.reshape(n_blocks, 1, rows)
    smem = lambda f: pl.BlockSpec((1, 1, rows), f, memory_space=pltpu.SMEM)
    return pl.pallas_call(
        _moe_body,
        grid_spec=pltpu.PrefetchScalarGridSpec(
            num_scalar_prefetch=1,
            grid=(n_blocks,),
            in_specs=[smem(lambda i, be: (i, 0, 0)),
                      smem(lambda i, be: (jnp.minimum(i + 1, n_blocks - 1), 0, 0)),
                      pl.BlockSpec(memory_space=pl.ANY),
                      pl.BlockSpec((None, 1, d, 2 * d_ff), lambda i, be: (layer, be[i], 0, 0)),
                      pl.BlockSpec((None, 1, d_ff, d), lambda i, be: (layer, be[i], 0, 0))],
            out_specs=pl.BlockSpec((rows, d), lambda i, be: (i, 0)),
            scratch_shapes=[pltpu.VMEM((2, rows, d), F32), pltpu.SemaphoreType.DMA((2,))]),
        out_shape=jax.ShapeDtypeStruct((n_blocks * rows, d), F32),
        compiler_params=_cparams("arbitrary"),
        name="moe_experts",
    )(blk_e, tok3, tok3, h, w13, w2)


def _combine_body(cur_ref, nxt_ref, x_ref, wt_ref, y_hbm, o_ref, ybuf, sem):
    i = pl.program_id(0)
    n = pl.num_programs(0)
    tm = x_ref.shape[0]
    slot = i % 2

    @pl.when(i == 0)
    def _():
        _row_gather_copy(y_hbm, cur_ref, 2 * tm, ybuf.at[0], sem.at[0])

    _row_gather_copy_unrolled(y_hbm, nxt_ref, 0, 2 * tm, ybuf.at[1 - slot], sem.at[1 - slot])
    _row_gather_wait(ybuf.at[slot], sem.at[slot])
    wt = wt_ref[...]
    o_ref[...] = x_ref[...] + wt[:, 0:1] * ybuf[slot, :tm] + wt[:, 1:2] * ybuf[slot, tm:]

    @pl.when(i == n - 1)
    def _():
        _row_gather_wait(ybuf.at[1 - slot], sem.at[1 - slot])


def _moe_combine(x1, wt, y, dest):
    t, d = x1.shape
    tm = TOKEN_TILE
    n = t // tm
    dest3 = dest.reshape(n, tm, 2).transpose(0, 2, 1).reshape(n, 1, 2 * tm)
    smem = lambda f: pl.BlockSpec((1, 1, 2 * tm), f, memory_space=pltpu.SMEM)
    row = lambda w: pl.BlockSpec((tm, w), lambda i: (i, 0))
    return pl.pallas_call(
        _combine_body,
        grid=(n,),
        in_specs=[smem(lambda i: (i, 0, 0)), smem(lambda i: (jnp.minimum(i + 1, n - 1), 0, 0)),
                  row(d), row(LANES), pl.BlockSpec(memory_space=pl.ANY)],
        out_specs=row(d),
        out_shape=jax.ShapeDtypeStruct((t, d), F32),
        scratch_shapes=[pltpu.VMEM((2, 2 * tm, d), F32), pltpu.SemaphoreType.DMA((2,))],
        compiler_params=_cparams("arbitrary"),
        name="moe_combine",
    )(dest3, dest3, x1, wt, y)


def _moe(x1, h, idx, wt, w13, w2, layer):
    t, d = h.shape
    top_i = idx[:, :TOP_K]
    flat_e = top_i.reshape(-1)
    onehot = (flat_e[:, None] == jnp.arange(N_EXPERTS, dtype=I32)[None, :]).astype(I32)
    csum = jnp.cumsum(onehot, axis=0)
    rank = jnp.take_along_axis(csum, flat_e[:, None], axis=1)[:, 0] - 1
    counts = csum[-1]
    padded = (counts + MOE_ROWS - 1) // MOE_ROWS * MOE_ROWS
    pad_end = jnp.cumsum(padded)
    pad_start = pad_end - padded
    dest = pad_start[flat_e] + rank
    n_blocks = -(-(t * TOP_K + N_EXPERTS * (MOE_ROWS - 1)) // MOE_ROWS)
    n_rows = n_blocks * MOE_ROWS
    flat_tok = jnp.arange(t * TOP_K, dtype=I32) // TOP_K
    row_tok = jnp.zeros((n_rows,), I32).at[dest].set(flat_tok)
    blk_e = jnp.minimum(jnp.searchsorted(pad_end, jnp.arange(n_blocks, dtype=I32) * MOE_ROWS, side="right"),
                        N_EXPERTS - 1).astype(I32)
    y = _moe_experts(h, row_tok, blk_e, w13, w2, layer)
    return _moe_combine(x1, wt, y, dest.reshape(t, TOP_K).astype(I32))


def _rope_tables(pos):
    half = ROT_DIM // 2
    inv_freq = ROPE_THETA ** (-jnp.arange(half, dtype=F32) / half)
    ang = pos.astype(F32)[:, None] * inv_freq[None, :]
    cos, sin = jnp.cos(ang), jnp.sin(ang)
    n = pos.shape[0]
    ones = jnp.ones((n, HEAD_DIM - ROT_DIM), F32)
    zeros8 = jnp.zeros((n, half), F32)
    zeros = jnp.zeros((n, HEAD_DIM - ROT_DIM), F32)
    c = jnp.concatenate([cos, cos, ones], axis=1)
    s1 = jnp.concatenate([-sin, zeros8, zeros], axis=1)
    s2 = jnp.concatenate([zeros8, sin, zeros], axis=1)
    return tuple(jnp.tile(a, (1, LANES // HEAD_DIM)) for a in (c, s1, s2))


def _layer_weights(l, p):
    d = p["w_in"].shape[1]
    w_in = p["w_in"][l]
    sizes = (A_W, A_W, A_W, B_QK_W, B_QK_W, B_V_W, C_QK_W, C_QK_W, C_V_W, C_GATE_RANK, C_V_W, M_W, N_BRANCH * d)
    offs = np.concatenate([[0], np.cumsum(sizes)])
    seg = lambda j: w_in[:, int(offs[j]):int(offs[j + 1])]
    ca_pad = jnp.pad(seg(9), ((0, 0), (0, LANES - C_GATE_RANK)))
    w_all = jnp.concatenate([seg(j) for j in (0, 1, 2, 3, 4, 5, 6, 7, 8, 10, 11, 12)] + [ca_pad], axis=1)
    blk = np.kron(np.eye(A_W // HEAD_DIM, dtype=np.float32), np.full((HEAD_DIM, HEAD_DIM), 1.0 / HEAD_DIM, np.float32))
    tile = lambda g, n: jnp.tile(g, n)[None, :].astype(F32)
    w_br = p["w_branch"][l].astype(BF16)
    o1, o2, o3 = A_W, A_W + B_V_W, A_W + B_V_W + C_V_W
    lw = {
        "norm1_g": p["norm1_g"][l][None, :], "w_in": w_all.astype(BF16), "bd": jnp.asarray(blk, BF16),
        "a_qn": tile(p["a_q_norm"][l], A_HEADS), "a_kn": tile(p["a_k_norm"][l], A_HEADS),
        "b_qn": tile(p["b_q_norm"][l], 2 * B_HEADS), "b_kn": tile(p["b_k_norm"][l], 2 * B_HEADS),
        "m_qn": tile(p["m_q_norm"][l], M_HEADS), "m_kn": tile(p["m_k_norm"][l], M_HEADS),
        "c_wa": jnp.pad(p["c_w_alpha"][l], ((0, LANES - C_GATE_RANK), (0, 0))).astype(BF16),
        "c_ba": p["c_b_alpha"][l][None, :],
        "mem_norm": p["mem_norm"][l][None, :], "w_mem_kv": p["w_mem_kv"][l].astype(BF16),
        "w_a": w_br[:o1], "w_b": w_br[o1:o2], "w_c": w_br[o2:o3], "w_m": w_br[o3:],
        "w_o": p["w_out"][l].astype(BF16), "norm2_g": p["norm2_g"][l][None, :],
        "b_lambda": p["b_lambda"][l], "b_subln": p["b_subln"][l][None, :], "c_on": p["c_out_norm"][l][None, :],
        "a_rel_bias": p["a_rel_bias"][l],
    }
    if l % 2 == 0:
        lw["ffn_w13"] = p["ffn_w13"][l // 2].astype(BF16)
        lw["ffn_w2"] = p["ffn_w2"][l // 2].astype(BF16)
    else:
        r = jnp.pad(p["moe_router"][l // 2], ((0, 0), (0, LANES - N_EXPERTS)))
        r_hi = r.astype(BF16)
        lw["router_hi"] = r_hi
        lw["router_lo"] = (r - r_hi.astype(F32)).astype(BF16)
        lw["moe_w13"] = p["moe_w13_bf16"]
        lw["moe_w2"] = p["moe_w2_bf16"]
    return lw


def _channel_mixer(l, x, br, lw):
    if l % 2 == 0:
        return _merge_ffn(x, br, lw)
    x1, h, idx, wt = _merge_router(x, br, lw)
    return _moe(x1, h, idx, wt, lw["moe_w13"], lw["moe_w2"], l // 2)


def kernel(x_prompt, x_sample, cache_a_k, cache_a_v, cache_b_k, cache_b_v, state_c, cache_mem_k, cache_mem_v,
           mem_prompt, norm1_g, w_in, a_q_norm, a_k_norm, a_rel_bias, b_q_norm, b_k_norm, b_lambda, b_subln,
           c_w_alpha, c_b_alpha, c_out_norm, mem_norm, w_mem_kv, m_q_norm, m_k_norm, w_branch, w_out, norm2_g,
           ffn_w13, ffn_w2, moe_router, moe_w13, moe_w2):
    p = dict(norm1_g=norm1_g, w_in=w_in, a_q_norm=a_q_norm, a_k_norm=a_k_norm, a_rel_bias=a_rel_bias,
             b_q_norm=b_q_norm, b_k_norm=b_k_norm, b_lambda=b_lambda, b_subln=b_subln, c_w_alpha=c_w_alpha,
             c_b_alpha=c_b_alpha, c_out_norm=c_out_norm, mem_norm=mem_norm, w_mem_kv=w_mem_kv, m_q_norm=m_q_norm,
             m_k_norm=m_k_norm, w_branch=w_branch, w_out=w_out, norm2_g=norm2_g, ffn_w13=ffn_w13, ffn_w2=ffn_w2,
             moe_router=moe_router, moe_w13_bf16=moe_w13.astype(BF16), moe_w2_bf16=moe_w2.astype(BF16))
    depth = w_in.shape[0]
    nb, seq, d = x_prompt.shape
    sb, t_new, _ = x_sample.shape
    past = cache_b_k.shape[2]
    n_mem = mem_prompt.shape[1]
    a_keep = min(A_PAST, seq)
    assert t_new == CHUNK and cache_a_k.shape[2] == A_PAST and seq % TOKEN_TILE == 0 and (sb * t_new) % TOKEN_TILE == 0

    rope_p = _rope_tables(jnp.tile(jnp.arange(seq), nb))
    rope_s = _rope_tables(jnp.tile(past + jnp.arange(t_new), sb))
    xp = x_prompt.reshape(nb * seq, d)
    xs = x_sample.reshape(sb * t_new, d)
    c_tile = min(C_TILE, seq)
    cache_bk = cache_b_k.reshape(depth, sb, past, B_QK_W)
    cache_bv = cache_b_v.reshape(depth, sb, past, B_V_W)
    outs = {k: [] for k in ("a_kp", "a_vp", "a_ks", "a_vs", "b_kp", "b_vp", "b_ks", "b_vs", "c_sp", "c_ss",
                            "m_kp", "m_vp")}

    for l in range(depth):
        lam_init = 0.8 - 0.6 * math.exp(-0.3 * l)
        lw = _layer_weights(l, p)

        pr = _inproj(xp, lw, rope_p)
        r3 = lambda a, b_=nb, s_=seq: a.reshape(b_, s_, a.shape[-1])
        br = {"gt": pr["gt"]}
        br["oa"] = _band_prompt(r3(pr["aq"]), r3(pr["ak"]), r3(pr["av"]), lw["a_rel_bias"]).reshape(nb * seq, A_W)
        br["ob"] = _diff_prompt(r3(pr["bq"]), r3(pr["bkh"]), r3(pr["bvh"]), lw["b_lambda"], lw["b_subln"],
                                lam_init).reshape(nb * seq, B_V_W)
        oc, c_fin = _gla(r3(pr["cq"]), r3(pr["ck"]), r3(pr["cv"]), r3(pr["cg"]), r3(pr["cr"]),
                         jnp.zeros((nb, C_QK_W, C_DV), F32), lw["c_on"], c_tile)
        br["oc"] = oc.reshape(nb * seq, C_V_W)
        mk, mv = _mem_kv(mem_prompt, lw)
        br["om"] = _dense_attention(r3(pr["mq"]), mk, mv, None, min(M_Q_TILE, seq)).reshape(nb * seq, M_W)
        outs["a_kp"].append(r3(pr["ak"])[:, seq - a_keep:].reshape(nb, a_keep, A_HEADS, HEAD_DIM))
        outs["a_vp"].append(r3(pr["av"])[:, seq - a_keep:].reshape(nb, a_keep, A_HEADS, HEAD_DIM))
        outs["b_kp"].append(pr["bk"].reshape(nb, seq, B_HEADS, 2, HEAD_DIM))
        outs["b_vp"].append(pr["bv"].reshape(nb, seq, B_HEADS, B_VDIM))
        outs["c_sp"].append(c_fin.reshape(nb, C_HEADS, C_DK, C_DV))
        outs["m_kp"].append(mk.reshape(nb, n_mem, M_HEADS, HEAD_DIM))
        outs["m_vp"].append(mv.reshape(nb, n_mem, M_HEADS, HEAD_DIM))
        xp = _channel_mixer(l, xp, br, lw)

        sr = _inproj(xs, lw, rope_s)
        s3 = lambda a: a.reshape(sb, t_new, a.shape[-1])
        bs = {"gt": sr["gt"]}
        ka = jnp.concatenate([cache_a_k[l].reshape(sb, A_PAST, A_W), s3(sr["ak"])], axis=1)
        va = jnp.concatenate([cache_a_v[l].reshape(sb, A_PAST, A_W), s3(sr["av"])], axis=1)
        bias_s = _rel_bias(lw["a_rel_bias"], t_new, A_PAST + t_new, A_PAST)
        bs["oa"] = _dense_attention(s3(sr["aq"]), ka, va, bias_s, t_new).reshape(sb * t_new, A_W)
        bs["ob"] = _diff_sample(s3(sr["bq"]), cache_bk, cache_bv, l, s3(sr["bkh"]), s3(sr["bvh"]),
                                lw["b_lambda"], lw["b_subln"], lam_init).reshape(sb * t_new, B_V_W)
        oc, c_new = _gla(s3(sr["cq"]), s3(sr["ck"]), s3(sr["cv"]), s3(sr["cg"]), s3(sr["cr"]),
                         state_c[l].reshape(sb, C_QK_W, C_DV), lw["c_on"], t_new)
        bs["oc"] = oc.reshape(sb * t_new, C_V_W)
        bs["om"] = _dense_attention(s3(sr["mq"]), cache_mem_k[l].reshape(sb, n_mem, M_W),
                                    cache_mem_v[l].reshape(sb, n_mem, M_W), None, t_new).reshape(sb * t_new, M_W)
        outs["a_ks"].append(sr["ak"].reshape(sb, t_new, A_HEADS, HEAD_DIM))
        outs["a_vs"].append(sr["av"].reshape(sb, t_new, A_HEADS, HEAD_DIM))
        outs["b_ks"].append(sr["bk"].reshape(sb, t_new, B_HEADS, 2, HEAD_DIM))
        outs["b_vs"].append(sr["bv"].reshape(sb, t_new, B_HEADS, B_VDIM))
        outs["c_ss"].append(c_new.reshape(sb, C_HEADS, C_DK, C_DV))
        xs = _channel_mixer(l, xs, bs, lw)

    st = {k: jnp.stack(v) for k, v in outs.items()}
    return (xp.reshape(nb, seq, d), xs.reshape(sb, t_new, d),
            st["a_kp"], st["a_vp"], st["a_ks"], st["a_vs"], st["b_kp"], st["b_vp"], st["b_ks"], st["b_vs"],
            st["c_sp"], st["c_ss"], st["m_kp"], st["m_vp"])
```

```python
import functools
import math

import numpy as np
import jax
import jax.numpy as jnp
from jax import lax
from jax.experimental import pallas as pl
from jax.experimental.pallas import tpu as pltpu

F32 = jnp.float32
BF16 = jnp.bfloat16
I32 = jnp.int32

CHUNK = 64
HEAD_DIM = 64
A_HEADS = 8
A_BAND_CHUNKS = 8
A_PAST = A_BAND_CHUNKS * CHUNK
REL_CLIP = 128
B_HEADS = 4
B_VDIM = 2 * HEAD_DIM
ROT_DIM = HEAD_DIM // 4
ROPE_THETA = 500000.0
C_HEADS = 4
C_DK = 64
C_DV = 128
C_GATE_RANK = 16
C_TAU = 16.0
M_HEADS = 4
N_BRANCH = 4
N_EXPERTS = 8
TOP_K = 2
EPS = 1e-6
NEG_INF = -1e30

A_W = A_HEADS * HEAD_DIM
B_QK_W = B_HEADS * 2 * HEAD_DIM
B_V_W = B_HEADS * B_VDIM
C_QK_W = C_HEADS * C_DK
C_V_W = C_HEADS * C_DV
M_W = M_HEADS * HEAD_DIM

LANES = 128
MXU_TILE = 256
VMEM_LIMIT_BYTES = 56 * 1024 * 1024

TOKEN_TILE = 256
A_Q_TILE = 256
B_Q_TILE = 1024
B_K_TILE = 1024
B_HEADS_PER_STEP = 2
B_CACHE_TILE = 2048
C_TILE = 256
M_Q_TILE = 512
FF_CHUNK = 1408
MOE_ROWS = 256
MOE_FF_CHUNK = 1792
QK_SCALE = HEAD_DIM ** -0.5
LOG2_E = math.log2(math.e)

_SEG = {}
_off = 0
for _name, _w in (("aq", A_W), ("ak", A_W), ("av", A_W), ("bq", B_QK_W), ("bk", B_QK_W), ("bv", B_V_W),
                  ("cq", C_QK_W), ("ck", C_QK_W), ("cv", C_V_W), ("cr", C_V_W), ("mq", M_W)):
    _SEG[_name] = (_off, _w)
    _off += _w
GATE_OFF = _off


def _cparams(*sem):
    return pltpu.CompilerParams(dimension_semantics=sem, vmem_limit_bytes=VMEM_LIMIT_BYTES)


def _const_spec(shape):
    nd = len(shape)
    return pl.BlockSpec(shape, lambda *_: (0,) * nd, pipeline_mode=pl.Buffered(1))


def _dot(a, b):
    return jnp.dot(a, b, preferred_element_type=F32)


def _dot_nt(a, b):
    return lax.dot_general(a, b, (((1,), (1,)), ((), ())), preferred_element_type=F32)


def _dot_tn(a, b):
    return lax.dot_general(a, b, (((0,), (0,)), ((), ())), preferred_element_type=F32)


def _rms(x, gain):
    return x * lax.rsqrt(jnp.mean(x * x, axis=-1, keepdims=True) + EPS) * gain


def _head_norm(y, bd_ref, gain):
    sq = (y * y).astype(BF16)
    ms = jnp.concatenate([_dot(sq[:, c:c + MXU_TILE], bd_ref[:MXU_TILE, :MXU_TILE])
                          for c in range(0, y.shape[-1], MXU_TILE)], axis=1)
    return y * lax.rsqrt(ms + EPS) * gain


def _sigmoid(x):
    return 1.0 / (1.0 + jnp.exp(-x))


def _inproj_body(x_ref, g1_ref, w_ref, bd_ref, aqn_ref, akn_ref, bqn_ref, bkn_ref, mqn_ref, cwa_ref, cba_ref,
                 rc_ref, rs1_ref, rs2_ref,
                 aq_o, ak_o, av_o, bq_o, bk_o, bkh_o, bv_o, bvh_o, cq_o, ck_o, cv_o, cg_o, cr_o, mq_o, gt_o):
    x = x_ref[...]
    h = _rms(x, g1_ref[...]).astype(BF16)
    d_model = x.shape[-1]

    def proj(name):
        lo, w = _SEG[name]
        return _dot(h, w_ref[:, lo:lo + w])

    def rope(y):
        reps = y.shape[-1] // LANES
        c = jnp.concatenate([rc_ref[...]] * reps, axis=1)
        s1 = jnp.concatenate([rs1_ref[...]] * reps, axis=1)
        s2 = jnp.concatenate([rs2_ref[...]] * reps, axis=1)
        half = ROT_DIM // 2
        return y * c + pltpu.roll(y, y.shape[-1] - half, 1) * s1 + pltpu.roll(y, half, 1) * s2

    aq_o[...] = (_head_norm(proj("aq"), bd_ref, aqn_ref[...]) * QK_SCALE).astype(BF16)
    ak_o[...] = _head_norm(proj("ak"), bd_ref, akn_ref[...])
    av_o[...] = proj("av")
    bq_o[...] = (rope(_head_norm(proj("bq"), bd_ref, bqn_ref[...])) * (QK_SCALE * LOG2_E)).astype(BF16)
    bk = rope(_head_norm(proj("bk"), bd_ref, bkn_ref[...]))
    bk_o[...] = bk
    bkh_o[...] = bk.astype(BF16)
    bv = proj("bv")
    bv_o[...] = bv
    bvh_o[...] = bv.astype(BF16)
    cq_o[...] = proj("cq") * (C_DK ** -0.5)
    ck_o[...] = proj("ck")
    cv_o[...] = proj("cv").astype(BF16)
    cr_o[...] = proj("cr").astype(BF16)
    mq_o[...] = (_head_norm(proj("mq"), bd_ref, mqn_ref[...]) * QK_SCALE).astype(BF16)
    ca = _dot(h, w_ref[:, GATE_OFF + N_BRANCH * d_model:])
    z = _dot(ca.astype(BF16), cwa_ref[...]) + cba_ref[...]
    cg_o[...] = (jnp.minimum(z, 0.0) - jnp.log1p(jnp.exp(-jnp.abs(z)))) * (1.0 / C_TAU)
    for c in range(N_BRANCH):
        lo = GATE_OFF + c * d_model
        gt_o[:, c * d_model:(c + 1) * d_model] = _sigmoid(_dot(h, w_ref[:, lo:lo + d_model])).astype(BF16)


def _inproj(x, lw, rope_tabs):
    t, d = x.shape
    tm = TOKEN_TILE
    wcols = lw["w_in"].shape[1]
    row = lambda w: pl.BlockSpec((tm, w), lambda i: (i, 0))
    outs = [("aq", A_W, BF16), ("ak", A_W, F32), ("av", A_W, F32), ("bq", B_QK_W, BF16), ("bk", B_QK_W, F32),
            ("bkh", B_QK_W, BF16), ("bv", B_V_W, F32), ("bvh", B_V_W, BF16), ("cq", C_QK_W, F32),
            ("ck", C_QK_W, F32), ("cv", C_V_W, BF16), ("cg", C_QK_W, F32), ("cr", C_V_W, BF16),
            ("mq", M_W, BF16), ("gt", N_BRANCH * d, BF16)]
    res = pl.pallas_call(
        _inproj_body,
        grid=(t // tm,),
        in_specs=[row(d), _const_spec((1, d)), _const_spec((d, wcols)), _const_spec((A_W, A_W)),
                  _const_spec((1, A_W)), _const_spec((1, A_W)), _const_spec((1, B_QK_W)), _const_spec((1, B_QK_W)),
                  _const_spec((1, M_W)), _const_spec((LANES, C_QK_W)), _const_spec((1, C_QK_W)),
                  row(LANES), row(LANES), row(LANES)],
        out_specs=[row(w) for _, w, _ in outs],
        out_shape=[jax.ShapeDtypeStruct((t, w), dt) for _, w, dt in outs],
        compiler_params=_cparams("parallel"),
        name="inproj",
    )(x, lw["norm1_g"], lw["w_in"], lw["bd"], lw["a_qn"], lw["a_kn"], lw["b_qn"], lw["b_kn"], lw["m_qn"],
      lw["c_wa"], lw["c_ba"], *rope_tabs)
    return {n: r for (n, _, _), r in zip(outs, res)}


def _pair_attention(q2, k2, v2, bias_even, bias_odd):
    tq = q2.shape[0]
    lane = lax.broadcasted_iota(I32, q2.shape, 1)
    zero = jnp.zeros_like(q2)
    qq = jnp.concatenate([jnp.where(lane < HEAD_DIM, q2, zero), jnp.where(lane >= HEAD_DIM, q2, zero)], axis=0)
    s = _dot_nt(qq, k2)
    if bias_even is not None:
        s = s + jnp.concatenate([bias_even, bias_odd], axis=0)
    m = jnp.max(s, axis=-1, keepdims=True)
    p = jnp.exp(s - m)
    l = jnp.sum(p, axis=-1, keepdims=True)
    o = _dot(p.astype(BF16), v2) / l
    return jnp.where(lane < HEAD_DIM, o[:tq], o[tq:])


def _band_body(q_ref, k0_ref, k1_ref, k2_ref, v0_ref, v1_ref, v2_ref, bias_ref, o_ref):
    i = pl.program_id(1)
    tq = q_ref.shape[1]
    k = jnp.concatenate([k0_ref[0], k1_ref[0], k2_ref[0]], axis=0).astype(BF16)
    v = jnp.concatenate([v0_ref[0], v1_ref[0], v2_ref[0]], axis=0).astype(BF16)
    col = lax.broadcasted_iota(I32, (tq, 3 * tq), 1)
    pad = jnp.where(col >= 2 * tq - i * tq, 0.0, NEG_INF)
    for p in range(A_HEADS // 2):
        sl = slice(p * LANES, (p + 1) * LANES)
        o_ref[0, :, sl] = _pair_attention(q_ref[0, :, sl], k[:, sl], v[:, sl],
                                          bias_ref[2 * p] + pad, bias_ref[2 * p + 1] + pad).astype(o_ref.dtype)


def _rel_bias(table, n_q, n_k, offset):
    period = n_q + n_k
    k = np.arange(period)
    dist = np.where(k < n_k, offset - k, offset - k + period)
    w = table[:, np.clip(dist, -REL_CLIP, REL_CLIP) + REL_CLIP].astype(F32)
    heads = table.shape[0]
    return jnp.tile(w, (1, n_q))[:, :n_q * (period - 1)].reshape(heads, n_q, period - 1)[:, :, :n_k]


def _band_bias(table, tq):
    r = np.arange(tq)[:, None]
    c = np.arange(3 * tq)[None, :]
    qc = (r + 2 * tq) // CHUNK
    kc = c // CHUNK
    vis = (kc <= qc) & (kc >= qc - A_BAND_CHUNKS)
    return jnp.where(jnp.asarray(vis)[None], _rel_bias(table, tq, 3 * tq, 2 * tq), NEG_INF)


def _band_prompt(aq, ak, av, table):
    b, s, w = aq.shape
    tq = A_Q_TILE
    assert A_PAST == 2 * tq and s % tq == 0
    qspec = pl.BlockSpec((1, tq, w), lambda bi, i: (bi, i, 0))
    kspec = lambda back: pl.BlockSpec((1, tq, w), lambda bi, i: (bi, jnp.maximum(i - back, 0), 0))
    return pl.pallas_call(
        _band_body,
        grid=(b, s // tq),
        in_specs=[qspec, kspec(2), kspec(1), kspec(0), kspec(2), kspec(1), kspec(0),
                  _const_spec((A_HEADS, tq, 3 * tq))],
        out_specs=qspec,
        out_shape=jax.ShapeDtypeStruct((b, s, w), BF16),
        compiler_params=_cparams("parallel", "parallel"),
        name="band_attention",
    )(aq, ak, ak, ak, av, av, av, _band_bias(table, tq))


def _dense_attn_body(*refs, heads, has_bias):
    if has_bias:
        q_ref, k_ref, v_ref, bias_ref, o_ref = refs
    else:
        q_ref, k_ref, v_ref, o_ref = refs
    k = k_ref[0].astype(BF16)
    v = v_ref[0].astype(BF16)
    for p in range(heads // 2):
        sl = slice(p * LANES, (p + 1) * LANES)
        be, bo = (bias_ref[2 * p], bias_ref[2 * p + 1]) if has_bias else (None, None)
        o_ref[0, :, sl] = _pair_attention(q_ref[0, :, sl], k[:, sl], v[:, sl], be, bo).astype(o_ref.dtype)


def _dense_attention(q, k, v, bias, tq):
    b, sq, w = q.shape
    sk = k.shape[1]
    heads = w // HEAD_DIM
    qspec = pl.BlockSpec((1, tq, w), lambda bi, i: (bi, i, 0))
    kspec = pl.BlockSpec((1, sk, w), lambda bi, i: (bi, 0, 0))
    in_specs = [qspec, kspec, kspec]
    args = [q, k, v]
    if bias is not None:
        in_specs.append(_const_spec(bias.shape))
        args.append(bias)
    return pl.pallas_call(
        functools.partial(_dense_attn_body, heads=heads, has_bias=bias is not None),
        grid=(b, sq // tq),
        in_specs=in_specs,
        out_specs=qspec,
        out_shape=jax.ShapeDtypeStruct((b, sq, w), BF16),
        compiler_params=_cparams("parallel", "parallel"),
        name="dense_attention",
    )(*args)


def _memkv_body(mem_ref, g_ref, w_ref, bd_ref, kn_ref, k_o, v_o):
    h = _rms(mem_ref[0], g_ref[...]).astype(BF16)
    kv = _dot(h, w_ref[...])
    k_o[0] = _head_norm(kv[:, :M_W], bd_ref, kn_ref[...])
    v_o[0] = kv[:, M_W:]


def _mem_kv(mem, lw):
    b, n, d = mem.shape
    spec = pl.BlockSpec((1, n, M_W), lambda bi: (bi, 0, 0))
    return pl.pallas_call(
        _memkv_body,
        grid=(b,),
        in_specs=[pl.BlockSpec((1, n, d), lambda bi: (bi, 0, 0)), _const_spec((1, d)), _const_spec((d, 2 * M_W)),
                  _const_spec((A_W, A_W)), _const_spec((1, M_W))],
        out_specs=[spec, spec],
        out_shape=[jax.ShapeDtypeStruct((b, n, M_W), F32)] * 2,
        compiler_params=_cparams("parallel"),
        name="mem_kv",
    )(mem, lw["mem_norm"], lw["w_mem_kv"], lw["bd"], lw["m_kn"])


def _diff_init(q, qq_scr, m_scr, l_scr, acc_scr):
    lane = lax.broadcasted_iota(I32, q.shape, 1)
    zero = jnp.zeros_like(q)
    tq = q.shape[0]
    qq_scr[:tq] = jnp.where(lane < HEAD_DIM, q, zero)
    qq_scr[tq:] = jnp.where(lane >= HEAD_DIM, q, zero)
    m_scr[...] = jnp.full(m_scr.shape, NEG_INF, F32)
    l_scr[...] = jnp.zeros(l_scr.shape, F32)
    acc_scr[...] = jnp.zeros(acc_scr.shape, F32)


def _diff_step(k, v, qq_scr, m_scr, l_scr, acc_scr, visible):
    s = _dot_nt(qq_scr[...], k)
    if visible is not None:
        s = jnp.where(visible, s, NEG_INF)
    m_prev = m_scr[...]
    m_new = jnp.maximum(m_prev, jnp.max(s, axis=-1, keepdims=True))
    alpha = jnp.exp2(m_prev - m_new)
    tk = s.shape[-1]
    m_wide = m_new[:, :tk] if tk <= LANES else jnp.concatenate([m_new] * (tk // LANES), axis=1)
    p = jnp.exp2(s - m_wide)
    l_scr[...] = alpha * l_scr[...] + jnp.sum(p, axis=-1, keepdims=True)
    acc_scr[...] = alpha * acc_scr[...] + _dot(p.astype(BF16), v)
    m_scr[...] = m_new


def _diff_finish(lam_ref, sub_ref, l_scr, acc_scr, lam_init):
    tq = acc_scr.shape[0] // 2
    lam_p = lam_ref[...]
    lam = (jnp.exp(jnp.sum(lam_p[0:1] * lam_p[1:2], axis=-1, keepdims=True))
           - jnp.exp(jnp.sum(lam_p[2:3] * lam_p[3:4], axis=-1, keepdims=True)) + lam_init)
    o = acc_scr[:tq] / l_scr[:tq] - lam * (acc_scr[tq:] / l_scr[tq:])
    return _rms(o, sub_ref[...]) * (1.0 - lam_init)


def _diff_scratch(tq):
    g = B_HEADS_PER_STEP
    return [pltpu.VMEM((g, 2 * tq, LANES), BF16), pltpu.VMEM((g, 2 * tq, LANES), F32),
            pltpu.VMEM((g, 2 * tq, LANES), F32), pltpu.VMEM((g, 2 * tq, B_VDIM), F32)]


def _head_lanes(g):
    return slice(g * LANES, (g + 1) * LANES)


def _diff_prompt_body(qi_ref, ki_ref, q_ref, k_ref, v_ref, lam_ref, sub_ref, o_ref,
                      qq_scr, m_scr, l_scr, acc_scr, *, lam_init):
    s_id = pl.program_id(2)
    qi = qi_ref[s_id]
    ki = ki_ref[s_id]
    tq = q_ref.shape[1]
    tk = k_ref.shape[1]
    heads = range(B_HEADS_PER_STEP)
    state = lambda g: (qq_scr.at[g], m_scr.at[g], l_scr.at[g], acc_scr.at[g])

    @pl.when(ki == 0)
    def _():
        for g in heads:
            _diff_init(q_ref[0, :, _head_lanes(g)], *state(g))

    needs_mask = (ki + 1) * tk > qi * tq

    @pl.when(needs_mask)
    def _():
        row = lax.broadcasted_iota(I32, (2 * tq, tk), 0)
        row = jnp.where(row >= tq, row - tq, row)
        col = lax.broadcasted_iota(I32, (2 * tq, tk), 1)
        chunk_shift = int(math.log2(CHUNK))
        visible = ((ki * tk + col) >> chunk_shift) <= ((qi * tq + row) >> chunk_shift)
        for g in heads:
            _diff_step(k_ref[0, :, _head_lanes(g)], v_ref[0, :, _head_lanes(g)], *state(g), visible)

    @pl.when(jnp.logical_not(needs_mask))
    def _():
        for g in heads:
            _diff_step(k_ref[0, :, _head_lanes(g)], v_ref[0, :, _head_lanes(g)], *state(g), None)

    last_ki = ((qi + 1) * tq - 1) // tk

    @pl.when(ki == last_ki)
    def _():
        for g in heads:
            o_ref[0, :, _head_lanes(g)] = _diff_finish(lam_ref, sub_ref, l_scr.at[g], acc_scr.at[g],
                                                       lam_init).astype(o_ref.dtype)


def _diff_prompt(bq, bkh, bvh, lam_p, sub, lam_init):
    b, s, w = bq.shape
    tq = min(B_Q_TILE, s)
    tk = min(B_K_TILE, s)
    gw = B_HEADS_PER_STEP * LANES
    steps = [(qi, ki) for qi in range(s // tq) for ki in range(((qi + 1) * tq - 1) // tk + 1)]
    qi_tab = jnp.asarray([p[0] for p in steps], I32)
    ki_tab = jnp.asarray([p[1] for p in steps], I32)
    qspec = pl.BlockSpec((1, tq, gw), lambda bi, h, st, qt, kt: (bi, qt[st], h))
    kspec = pl.BlockSpec((1, tk, gw), lambda bi, h, st, qt, kt: (bi, kt[st], h))
    cspec = lambda shp: pl.BlockSpec(shp, lambda bi, h, st, qt, kt: (0, 0))
    return pl.pallas_call(
        functools.partial(_diff_prompt_body, lam_init=lam_init),
        grid_spec=pltpu.PrefetchScalarGridSpec(
            num_scalar_prefetch=2,
            grid=(b, B_HEADS // B_HEADS_PER_STEP, len(steps)),
            in_specs=[qspec, kspec, kspec, cspec((4, HEAD_DIM)), cspec((1, B_VDIM))],
            out_specs=qspec,
            scratch_shapes=_diff_scratch(tq)),
        out_shape=jax.ShapeDtypeStruct((b, s, w), BF16),
        compiler_params=_cparams("parallel", "parallel", "arbitrary"),
        name="diff_attention",
    )(qi_tab, ki_tab, bq, bkh, bvh, lam_p, sub)


def _diff_sample_body(q_ref, ck_ref, cv_ref, nk_ref, nv_ref, lam_ref, sub_ref, o_ref,
                      qq_scr, m_scr, l_scr, acc_scr, *, lam_init):
    t = pl.program_id(2)
    heads = range(B_HEADS_PER_STEP)
    state = lambda g: (qq_scr.at[g], m_scr.at[g], l_scr.at[g], acc_scr.at[g])

    @pl.when(t == 0)
    def _():
        for g in heads:
            _diff_init(q_ref[0, :, _head_lanes(g)], *state(g))

    for g in heads:
        _diff_step(ck_ref[0, :, _head_lanes(g)].astype(BF16), cv_ref[0, :, _head_lanes(g)].astype(BF16),
                   *state(g), None)

    @pl.when(t == pl.num_programs(2) - 1)
    def _():
        for g in heads:
            _diff_step(nk_ref[0, :, _head_lanes(g)], nv_ref[0, :, _head_lanes(g)], *state(g), None)
            o_ref[0, :, _head_lanes(g)] = _diff_finish(lam_ref, sub_ref, l_scr.at[g], acc_scr.at[g],
                                                       lam_init).astype(o_ref.dtype)


def _diff_sample(bq, cache_k, cache_v, layer, bkh, bvh, lam_p, sub, lam_init):
    b, tq, w = bq.shape
    past = cache_k.shape[2]
    tc = min(B_CACHE_TILE, past)
    gw = B_HEADS_PER_STEP * LANES
    assert past % tc == 0 and tq == CHUNK and past % CHUNK == 0
    qspec = pl.BlockSpec((1, tq, gw), lambda bi, h, t: (bi, 0, h))
    cspec = pl.BlockSpec((None, 1, tc, gw), lambda bi, h, t: (layer, bi, t, h))
    pspec = lambda shp: pl.BlockSpec(shp, lambda bi, h, t: (0, 0))
    return pl.pallas_call(
        functools.partial(_diff_sample_body, lam_init=lam_init),
        grid=(b, B_HEADS // B_HEADS_PER_STEP, past // tc),
        in_specs=[qspec, cspec, cspec, qspec, qspec, pspec((4, HEAD_DIM)), pspec((1, B_VDIM))],
        out_specs=qspec,
        scratch_shapes=_diff_scratch(tq),
        out_shape=jax.ShapeDtypeStruct((b, tq, w), BF16),
        compiler_params=_cparams("parallel", "parallel", "arbitrary"),
        name="diff_attention_sample",
    )(bq, cache_k, cache_v, bkh, bvh, lam_p, sub)


def _gla_decay_matrix(t):
    levels = int(math.log2(t))
    d = np.zeros((levels + 2, t, t), np.float32)
    u = np.arange(t)[None, :]
    i = np.arange(t)[:, None]
    for lv in range(levels):
        g = t >> lv
        base = i - i % g
        r = base + g // 2 - 1
        second = (i % g) >= g // 2
        d[lv] = np.where(second, (u > r) & (u <= i), (u > i) & (u <= r))
    d[levels] = u <= i
    d[levels + 1] = u > i
    return jnp.asarray(d.reshape((levels + 2) * t, t), BF16)


def _gla_body(q_ref, k_ref, v_ref, g_ref, r_ref, s0_ref, d_ref, on_ref, o_ref, sf_ref, s_scr):
    c = pl.program_id(1)
    t = q_ref.shape[1]
    levels = int(math.log2(t))

    @pl.when(c == 0)
    def _():
        s_scr[...] = s0_ref[0]

    q = q_ref[0]
    k = k_ref[0]
    g = g_ref[0]
    g_hi = g.astype(BF16)
    g_lo = (g - g_hi.astype(F32)).astype(BF16)
    g2 = jnp.concatenate([g_hi, g_lo], axis=1)

    def exponent(block):
        e = _dot(d_ref[block * t:(block + 1) * t, :], g2)
        return e[:, :C_QK_W] + e[:, C_QK_W:]

    lane = lax.broadcasted_iota(I32, (t, C_QK_W), 1)
    head_lanes = [(lane >= h * C_DK) & (lane < (h + 1) * C_DK) for h in range(C_HEADS)]
    tok = lax.broadcasted_iota(I32, (t, C_QK_W), 0)
    ri = lax.broadcasted_iota(I32, (t, t), 0)
    ci = lax.broadcasted_iota(I32, (t, t), 1)
    zero = jnp.zeros((t, C_QK_W), BF16)
    kb = k.astype(BF16)
    qb = q.astype(BF16)

    att = [jnp.where(ri == ci, _dot_nt(jnp.where(head_lanes[h], qb, zero), kb), 0.0) for h in range(C_HEADS)]
    for lv in range(levels):
        shift = levels - lv
        w = jnp.exp(exponent(lv))
        second = ((tok >> (shift - 1)) & 1) == 1
        ql = jnp.where(second, q * w, 0.0).astype(BF16)
        kl = jnp.where(second, 0.0, k * w).astype(BF16)
        same = (ri >> shift) == (ci >> shift)
        for h in range(C_HEADS):
            att[h] = att[h] + jnp.where(same, _dot_nt(jnp.where(head_lanes[h], ql, zero), kl), 0.0)

    state = s_scr[...]
    state_b = state.astype(BF16)
    q_in = (q * jnp.exp(exponent(levels))).astype(BF16)
    k_out = (k * jnp.exp(exponent(levels + 1))).astype(BF16)
    ones = jnp.ones((t, C_DV), BF16)
    decay = jnp.exp(_dot_tn(g_hi, ones) + _dot_tn(g_lo, ones))
    srow = lax.broadcasted_iota(I32, (C_QK_W, C_DV), 0)
    new_state = decay * state
    for h in range(C_HEADS):
        sl = slice(h * C_DV, (h + 1) * C_DV)
        v = v_ref[0, :, sl]
        o = _dot(att[h].astype(BF16), v) + _dot(jnp.where(head_lanes[h], q_in, zero), state_b)
        r = r_ref[0, :, sl].astype(F32)
        o_ref[0, :, sl] = (_rms(o, on_ref[...]) * (r * _sigmoid(r))).astype(o_ref.dtype)
        upd = _dot_tn(k_out, v)
        new_state = new_state + jnp.where((srow >= h * C_DK) & (srow < (h + 1) * C_DK), upd, 0.0)
    s_scr[...] = new_state

    @pl.when(c == pl.num_programs(1) - 1)
    def _():
        sf_ref[0] = new_state


def _gla(cq, ck, cv, cg, cr, state0, c_on, t):
    b, s, _ = cq.shape
    levels = int(math.log2(t))
    assert 1 << levels == t and s % t == 0
    tok = lambda w: pl.BlockSpec((1, t, w), lambda bi, c: (bi, c, 0))
    sspec = pl.BlockSpec((1, C_QK_W, C_DV), lambda bi, c: (bi, 0, 0))
    return pl.pallas_call(
        _gla_body,
        grid=(b, s // t),
        in_specs=[tok(C_QK_W), tok(C_QK_W), tok(C_V_W), tok(C_QK_W), tok(C_V_W), sspec,
                  _const_spec(((levels + 2) * t, t)), _const_spec((1, C_DV))],
        out_specs=[tok(C_V_W), sspec],
        out_shape=[jax.ShapeDtypeStruct((b, s, C_V_W), BF16), jax.ShapeDtypeStruct((b, C_QK_W, C_DV), F32)],
        scratch_shapes=[pltpu.VMEM((C_QK_W, C_DV), F32)],
        compiler_params=_cparams("parallel", "arbitrary"),
        name="gla",
    )(cq, ck, cv, cg, cr, state0, _gla_decay_matrix(t), c_on)


def _merge_residual(x_ref, oa_ref, ob_ref, oc_ref, om_ref, gt_ref, wa_ref, wb_ref, wc_ref, wm_ref, wo_ref):
    d = x_ref.shape[-1]
    gate = lambda c: gt_ref[:, c * d:(c + 1) * d].astype(F32)
    y = (gate(0) * _dot(oa_ref[...], wa_ref[...]) + gate(1) * _dot(ob_ref[...], wb_ref[...])
         + gate(2) * _dot(oc_ref[...], wc_ref[...]) + gate(3) * _dot(om_ref[...], wm_ref[...]))
    return x_ref[...] + _dot(y.astype(BF16), wo_ref[...])


def _merge_ffn_body(x_ref, oa_ref, ob_ref, oc_ref, om_ref, gt_ref, wa_ref, wb_ref, wc_ref, wm_ref, wo_ref,
                    g2_ref, w13_ref, w2_ref, o_ref):
    x1 = _merge_residual(x_ref, oa_ref, ob_ref, oc_ref, om_ref, gt_ref, wa_ref, wb_ref, wc_ref, wm_ref, wo_ref)
    h = _rms(x1, g2_ref[...]).astype(BF16)
    d_ff = w2_ref.shape[0]
    acc = x1
    for c in range(d_ff // FF_CHUNK):
        lo = c * FF_CHUNK
        a = _dot(h, w13_ref[:, lo:lo + FF_CHUNK])
        b = _dot(h, w13_ref[:, d_ff + lo:d_ff + lo + FF_CHUNK])
        acc = acc + _dot((a * _sigmoid(a) * b).astype(BF16), w2_ref[lo:lo + FF_CHUNK, :])
    o_ref[...] = acc


def _merge_router_body(x_ref, oa_ref, ob_ref, oc_ref, om_ref, gt_ref, wa_ref, wb_ref, wc_ref, wm_ref, wo_ref,
                       g2_ref, rh_ref, rl_ref, x1_o, h_o, idx_o, wt_o):
    x1 = _merge_residual(x_ref, oa_ref, ob_ref, oc_ref, om_ref, gt_ref, wa_ref, wb_ref, wc_ref, wm_ref, wo_ref)
    x1_o[...] = x1
    h = _rms(x1, g2_ref[...])
    h_o[...] = h
    h_hi = h.astype(BF16)
    h_lo = (h - h_hi.astype(F32)).astype(BF16)
    logits = _dot(h_hi, rh_ref[...]) + _dot(h_lo, rh_ref[...]) + _dot(h_hi, rl_ref[...])
    lane = lax.broadcasted_iota(I32, logits.shape, 1)
    lane_f = lane.astype(F32)
    neg = jnp.float32(-jnp.inf)
    lg = jnp.where(lane < N_EXPERTS, logits, neg)
    m1 = jnp.max(lg, axis=-1, keepdims=True)
    i1 = jnp.min(jnp.where(lg == m1, lane_f, float(LANES)), axis=-1, keepdims=True)
    lg2 = jnp.where(lane_f == i1, neg, lg)
    m2 = jnp.max(lg2, axis=-1, keepdims=True)
    i2 = jnp.min(jnp.where(lg2 == m2, lane_f, float(LANES)), axis=-1, keepdims=True)
    e = jnp.exp(m2 - m1)
    w1 = 1.0 / (1.0 + e)
    idx_o[...] = jnp.where(lane == 0, i1, jnp.where(lane == 1, i2, 0.0)).astype(I32)
    wt_o[...] = jnp.where(lane == 0, w1, jnp.where(lane == 1, e * w1, 0.0))


def _merge_specs(t, d):
    tm = TOKEN_TILE
    row = lambda w: pl.BlockSpec((tm, w), lambda i: (i, 0))
    specs = [row(d), row(A_W), row(B_V_W), row(C_V_W), row(M_W), row(N_BRANCH * d),
             _const_spec((A_W, d)), _const_spec((B_V_W, d)), _const_spec((C_V_W, d)), _const_spec((M_W, d)),
             _const_spec((d, d)), _const_spec((1, d))]
    return tm, row, specs


def _merge_args(x, br, lw):
    return (x, br["oa"], br["ob"], br["oc"], br["om"], br["gt"], lw["w_a"], lw["w_b"], lw["w_c"], lw["w_m"],
            lw["w_o"], lw["norm2_g"])


def _merge_ffn(x, br, lw):
    t, d = x.shape
    tm, row, specs = _merge_specs(t, d)
    d_ff = lw["ffn_w2"].shape[0]
    assert d_ff % FF_CHUNK == 0
    return pl.pallas_call(
        _merge_ffn_body,
        grid=(t // tm,),
        in_specs=specs + [_const_spec((d, 2 * d_ff)), _const_spec((d_ff, d))],
        out_specs=row(d),
        out_shape=jax.ShapeDtypeStruct((t, d), F32),
        compiler_params=_cparams("parallel"),
        name="merge_ffn",
    )(*_merge_args(x, br, lw), lw["ffn_w13"], lw["ffn_w2"])


def _merge_router(x, br, lw):
    t, d = x.shape
    tm, row, specs = _merge_specs(t, d)
    return pl.pallas_call(
        _merge_router_body,
        grid=(t // tm,),
        in_specs=specs + [_const_spec((d, LANES)), _const_spec((d, LANES))],
        out_specs=[row(d), row(d), row(LANES), row(LANES)],
        out_shape=[jax.ShapeDtypeStruct((t, d), F32), jax.ShapeDtypeStruct((t, d), F32),
                   jax.ShapeDtypeStruct((t, LANES), I32), jax.ShapeDtypeStruct((t, LANES), F32)],
        compiler_params=_cparams("parallel"),
        name="merge_router",
    )(*_merge_args(x, br, lw), lw["router_hi"], lw["router_lo"])


def _row_gather_copy(src_hbm, idx_ref, n, dst, sem):
    def issue(r, carry):
        pltpu.make_async_copy(src_hbm.at[pl.ds(idx_ref[0, 0, r], 1)], dst.at[pl.ds(r, 1)], sem).start()
        return carry
    lax.fori_loop(0, n, issue, 0, unroll=8)


def _row_gather_copy_unrolled(src_hbm, idx_ref, lo, hi, dst, sem):
    for r in range(lo, hi):
        pltpu.make_async_copy(src_hbm.at[pl.ds(idx_ref[0, 0, r], 1)], dst.at[pl.ds(r, 1)], sem).start(priority=r % 2)


def _row_gather_wait(dst, sem):
    pltpu.make_async_copy(dst, dst, sem).wait()


def _moe_body(blk_e_ref, cur_ref, nxt_ref, h_hbm, w13_ref, w2_ref, y_ref, xbuf, sem):
    i = pl.program_id(0)
    n = pl.num_programs(0)
    rows = xbuf.shape[1]
    slot = i % 2

    @pl.when(i == 0)
    def _():
        _row_gather_copy(h_hbm, cur_ref, rows, xbuf.at[0], sem.at[0])

    @pl.when(i + 1 < n)
    def _():
        _row_gather_copy_unrolled(h_hbm, nxt_ref, 0, rows, xbuf.at[1 - slot], sem.at[1 - slot])

    _row_gather_wait(xbuf.at[slot], sem.at[slot])
    x = xbuf[slot].astype(BF16)
    d_ff = w2_ref.shape[1]
    acc = jnp.zeros(y_ref.shape, F32)
    for c in range(d_ff // MOE_FF_CHUNK):
        lo = c * MOE_FF_CHUNK
        a = _dot(x, w13_ref[0, :, lo:lo + MOE_FF_CHUNK])
        b = _dot(x, w13_ref[0, :, d_ff + lo:d_ff + lo + MOE_FF_CHUNK])
        acc = acc + _dot((a * _sigmoid(a) * b).astype(BF16), w2_ref[0, lo:lo + MOE_FF_CHUNK, :])
    y_ref[...] = acc


def _moe_experts(h, row_tok, blk_e, w13, w2, layer):
    t, d = h.shape
    n_blocks = blk_e.shape[0]
    rows = MOE_ROWS
    d_ff = w2.shape[2]
    assert d_ff % MOE_FF_CHUNK == 0
    tok3 = row_tok.reshape(n_blocks, 1, rows)
    smem = lambda f: pl.BlockSpec((1, 1, rows), f, memory_space=pltpu.SMEM)
    return pl.pallas_call(
        _moe_body,
        grid_spec=pltpu.PrefetchScalarGridSpec(
            num_scalar_prefetch=1,
            grid=(n_blocks,),
            in_specs=[smem(lambda i, be: (i, 0, 0)),
                      smem(lambda i, be: (jnp.minimum(i + 1, n_blocks - 1), 0, 0)),
                      pl.BlockSpec(memory_space=pl.ANY),
                      pl.BlockSpec((None, 1, d, 2 * d_ff), lambda i, be: (layer, be[i], 0, 0)),
                      pl.BlockSpec((None, 1, d_ff, d), lambda i, be: (layer, be[i], 0, 0))],
            out_specs=pl.BlockSpec((rows, d), lambda i, be: (i, 0)),
            scratch_shapes=[pltpu.VMEM((2, rows, d), F32), pltpu.SemaphoreType.DMA((2,))]),
        out_shape=jax.ShapeDtypeStruct((n_blocks * rows, d), F32),
        compiler_params=_cparams("arbitrary"),
        name="moe_experts",
    )(blk_e, tok3, tok3, h, w13, w2)


def _combine_body(cur_ref, nxt_ref, x_ref, wt_ref, y_hbm, o_ref, ybuf, sem):
    i = pl.program_id(0)
    n = pl.num_programs(0)
    tm = x_ref.shape[0]
    slot = i % 2

    @pl.when(i == 0)
    def _():
        _row_gather_copy(y_hbm, cur_ref, 2 * tm, ybuf.at[0], sem.at[0])

    _row_gather_copy_unrolled(y_hbm, nxt_ref, 0, 2 * tm, ybuf.at[1 - slot], sem.at[1 - slot])
    _row_gather_wait(ybuf.at[slot], sem.at[slot])
    wt = wt_ref[...]
    o_ref[...] = x_ref[...] + wt[:, 0:1] * ybuf[slot, :tm] + wt[:, 1:2] * ybuf[slot, tm:]

    @pl.when(i == n - 1)
    def _():
        _row_gather_wait(ybuf.at[1 - slot], sem.at[1 - slot])


def _moe_combine(x1, wt, y, dest):
    t, d = x1.shape
    tm = TOKEN_TILE
    n = t // tm
    dest3 = dest.reshape(n, tm, 2).transpose(0, 2, 1).reshape(n, 1, 2 * tm)
    smem = lambda f: pl.BlockSpec((1, 1, 2 * tm), f, memory_space=pltpu.SMEM)
    row = lambda w: pl.BlockSpec((tm, w), lambda i: (i, 0))
    return pl.pallas_call(
        _combine_body,
        grid=(n,),
        in_specs=[smem(lambda i: (i, 0, 0)), smem(lambda i: (jnp.minimum(i + 1, n - 1), 0, 0)),
                  row(d), row(LANES), pl.BlockSpec(memory_space=pl.ANY)],
        out_specs=row(d),
        out_shape=jax.ShapeDtypeStruct((t, d), F32),
        scratch_shapes=[pltpu.VMEM((2, 2 * tm, d), F32), pltpu.SemaphoreType.DMA((2,))],
        compiler_params=_cparams("arbitrary"),
        name="moe_combine",
    )(dest3, dest3, x1, wt, y)


def _moe(x1, h, idx, wt, w13, w2, layer):
    t, d = h.shape
    top_i = idx[:, :TOP_K]
    flat_e = top_i.reshape(-1)
    onehot = (flat_e[:, None] == jnp.arange(N_EXPERTS, dtype=I32)[None, :]).astype(I32)
    csum = jnp.cumsum(onehot, axis=0)
    rank = jnp.take_along_axis(csum, flat_e[:, None], axis=1)[:, 0] - 1
    counts = csum[-1]
    padded = (counts + MOE_ROWS - 1) // MOE_ROWS * MOE_ROWS
    pad_end = jnp.cumsum(padded)
    pad_start = pad_end - padded
    dest = pad_start[flat_e] + rank
    n_blocks = -(-(t * TOP_K + N_EXPERTS * (MOE_ROWS - 1)) // MOE_ROWS)
    n_rows = n_blocks * MOE_ROWS
    flat_tok = jnp.arange(t * TOP_K, dtype=I32) // TOP_K
    row_tok = jnp.zeros((n_rows,), I32).at[dest].set(flat_tok)
    blk_e = jnp.minimum(jnp.searchsorted(pad_end, jnp.arange(n_blocks, dtype=I32) * MOE_ROWS, side="right"),
                        N_EXPERTS - 1).astype(I32)
    y = _moe_experts(h, row_tok, blk_e, w13, w2, layer)
    return _moe_combine(x1, wt, y, dest.reshape(t, TOP_K).astype(I32))


def _rope_tables(pos):
    half = ROT_DIM // 2
    inv_freq = ROPE_THETA ** (-jnp.arange(half, dtype=F32) / half)
    ang = pos.astype(F32)[:, None] * inv_freq[None, :]
    cos, sin = jnp.cos(ang), jnp.sin(ang)
    n = pos.shape[0]
    ones = jnp.ones((n, HEAD_DIM - ROT_DIM), F32)
    zeros8 = jnp.zeros((n, half), F32)
    zeros = jnp.zeros((n, HEAD_DIM - ROT_DIM), F32)
    c = jnp.concatenate([cos, cos, ones], axis=1)
    s1 = jnp.concatenate([-sin, zeros8, zeros], axis=1)
    s2 = jnp.concatenate([zeros8, sin, zeros], axis=1)
    return tuple(jnp.tile(a, (1, LANES // HEAD_DIM)) for a in (c, s1, s2))


def _layer_weights(l, p):
    d = p["w_in"].shape[1]
    w_in = p["w_in"][l]
    sizes = (A_W, A_W, A_W, B_QK_W, B_QK_W, B_V_W, C_QK_W, C_QK_W, C_V_W, C_GATE_RANK, C_V_W, M_W, N_BRANCH * d)
    offs = np.concatenate([[0], np.cumsum(sizes)])
    seg = lambda j: w_in[:, int(offs[j]):int(offs[j + 1])]
    ca_pad = jnp.pad(seg(9), ((0, 0), (0, LANES - C_GATE_RANK)))
    w_all = jnp.concatenate([seg(j) for j in (0, 1, 2, 3, 4, 5, 6, 7, 8, 10, 11, 12)] + [ca_pad], axis=1)
    blk = np.kron(np.eye(A_W // HEAD_DIM, dtype=np.float32), np.full((HEAD_DIM, HEAD_DIM), 1.0 / HEAD_DIM, np.float32))
    tile = lambda g, n: jnp.tile(g, n)[None, :].astype(F32)
    w_br = p["w_branch"][l].astype(BF16)
    o1, o2, o3 = A_W, A_W + B_V_W, A_W + B_V_W + C_V_W
    lw = {
        "norm1_g": p["norm1_g"][l][None, :], "w_in": w_all.astype(BF16), "bd": jnp.asarray(blk, BF16),
        "a_qn": tile(p["a_q_norm"][l], A_HEADS), "a_kn": tile(p["a_k_norm"][l], A_HEADS),
        "b_qn": tile(p["b_q_norm"][l], 2 * B_HEADS), "b_kn": tile(p["b_k_norm"][l], 2 * B_HEADS),
        "m_qn": tile(p["m_q_norm"][l], M_HEADS), "m_kn": tile(p["m_k_norm"][l], M_HEADS),
        "c_wa": jnp.pad(p["c_w_alpha"][l], ((0, LANES - C_GATE_RANK), (0, 0))).astype(BF16),
        "c_ba": p["c_b_alpha"][l][None, :],
        "mem_norm": p["mem_norm"][l][None, :], "w_mem_kv": p["w_mem_kv"][l].astype(BF16),
        "w_a": w_br[:o1], "w_b": w_br[o1:o2], "w_c": w_br[o2:o3], "w_m": w_br[o3:],
        "w_o": p["w_out"][l].astype(BF16), "norm2_g": p["norm2_g"][l][None, :],
        "b_lambda": p["b_lambda"][l], "b_subln": p["b_subln"][l][None, :], "c_on": p["c_out_norm"][l][None, :],
        "a_rel_bias": p["a_rel_bias"][l],
    }
    if l % 2 == 0:
        lw["ffn_w13"] = p["ffn_w13"][l // 2].astype(BF16)
        lw["ffn_w2"] = p["ffn_w2"][l // 2].astype(BF16)
    else:
        r = jnp.pad(p["moe_router"][l // 2], ((0, 0), (0, LANES - N_EXPERTS)))
        r_hi = r.astype(BF16)
        lw["router_hi"] = r_hi
        lw["router_lo"] = (r - r_hi.astype(F32)).astype(BF16)
        lw["moe_w13"] = p["moe_w13_bf16"]
        lw["moe_w2"] = p["moe_w2_bf16"]
    return lw


def _channel_mixer(l, x, br, lw):
    if l % 2 == 0:
        return _merge_ffn(x, br, lw)
    x1, h, idx, wt = _merge_router(x, br, lw)
    return _moe(x1, h, idx, wt, lw["moe_w13"], lw["moe_w2"], l // 2)


def kernel(x_prompt, x_sample, cache_a_k, cache_a_v, cache_b_k, cache_b_v, state_c, cache_mem_k, cache_mem_v,
           mem_prompt, norm1_g, w_in, a_q_norm, a_k_norm, a_rel_bias, b_q_norm, b_k_norm, b_lambda, b_subln,
           c_w_alpha, c_b_alpha, c_out_norm, mem_norm, w_mem_kv, m_q_norm, m_k_norm, w_branch, w_out, norm2_g,
           ffn_w13, ffn_w2, moe_router, moe_w13, moe_w2):
    p = dict(norm1_g=norm1_g, w_in=w_in, a_q_norm=a_q_norm, a_k_norm=a_k_norm, a_rel_bias=a_rel_bias,
             b_q_norm=b_q_norm, b_k_norm=b_k_norm, b_lambda=b_lambda, b_subln=b_subln, c_w_alpha=c_w_alpha,
             c_b_alpha=c_b_alpha, c_out_norm=c_out_norm, mem_norm=mem_norm, w_mem_kv=w_mem_kv, m_q_norm=m_q_norm,
             m_k_norm=m_k_norm, w_branch=w_branch, w_out=w_out, norm2_g=norm2_g, ffn_w13=ffn_w13, ffn_w2=ffn_w2,
             moe_router=moe_router, moe_w13_bf16=moe_w13.astype(BF16), moe_w2_bf16=moe_w2.astype(BF16))
    depth = w_in.shape[0]
    nb, seq, d = x_prompt.shape
    sb, t_new, _ = x_sample.shape
    past = cache_b_k.shape[2]
    n_mem = mem_prompt.shape[1]
    a_keep = min(A_PAST, seq)
    assert t_new == CHUNK and cache_a_k.shape[2] == A_PAST and seq % TOKEN_TILE == 0 and (sb * t_new) % TOKEN_TILE == 0

    rope_p = _rope_tables(jnp.tile(jnp.arange(seq), nb))
    rope_s = _rope_tables(jnp.tile(past + jnp.arange(t_new), sb))
    xp = x_prompt.reshape(nb * seq, d)
    xs = x_sample.reshape(sb * t_new, d)
    c_tile = min(C_TILE, seq)
    cache_bk = cache_b_k.reshape(depth, sb, past, B_QK_W)
    cache_bv = cache_b_v.reshape(depth, sb, past, B_V_W)
    outs = {k: [] for k in ("a_kp", "a_vp", "a_ks", "a_vs", "b_kp", "b_vp", "b_ks", "b_vs", "c_sp", "c_ss",
                            "m_kp", "m_vp")}

    for l in range(depth):
        lam_init = 0.8 - 0.6 * math.exp(-0.3 * l)
        lw = _layer_weights(l, p)

        pr = _inproj(xp, lw, rope_p)
        r3 = lambda a, b_=nb, s_=seq: a.reshape(b_, s_, a.shape[-1])
        br = {"gt": pr["gt"]}
        br["oa"] = _band_prompt(r3(pr["aq"]), r3(pr["ak"]), r3(pr["av"]), lw["a_rel_bias"]).reshape(nb * seq, A_W)
        br["ob"] = _diff_prompt(r3(pr["bq"]), r3(pr["bkh"]), r3(pr["bvh"]), lw["b_lambda"], lw["b_subln"],
                                lam_init).reshape(nb * seq, B_V_W)
        oc, c_fin = _gla(r3(pr["cq"]), r3(pr["ck"]), r3(pr["cv"]), r3(pr["cg"]), r3(pr["cr"]),
                         jnp.zeros((nb, C_QK_W, C_DV), F32), lw["c_on"], c_tile)
        br["oc"] = oc.reshape(nb * seq, C_V_W)
        mk, mv = _mem_kv(mem_prompt, lw)
        br["om"] = _dense_attention(r3(pr["mq"]), mk, mv, None, min(M_Q_TILE, seq)).reshape(nb * seq, M_W)
        outs["a_kp"].append(r3(pr["ak"])[:, seq - a_keep:].reshape(nb, a_keep, A_HEADS, HEAD_DIM))
        outs["a_vp"].append(r3(pr["av"])[:, seq - a_keep:].reshape(nb, a_keep, A_HEADS, HEAD_DIM))
        outs["b_kp"].append(pr["bk"].reshape(nb, seq, B_HEADS, 2, HEAD_DIM))
        outs["b_vp"].append(pr["bv"].reshape(nb, seq, B_HEADS, B_VDIM))
        outs["c_sp"].append(c_fin.reshape(nb, C_HEADS, C_DK, C_DV))
        outs["m_kp"].append(mk.reshape(nb, n_mem, M_HEADS, HEAD_DIM))
        outs["m_vp"].append(mv.reshape(nb, n_mem, M_HEADS, HEAD_DIM))
        xp = _channel_mixer(l, xp, br, lw)

        sr = _inproj(xs, lw, rope_s)
        s3 = lambda a: a.reshape(sb, t_new, a.shape[-1])
        bs = {"gt": sr["gt"]}
        ka = jnp.concatenate([cache_a_k[l].reshape(sb, A_PAST, A_W), s3(sr["ak"])], axis=1)
        va = jnp.concatenate([cache_a_v[l].reshape(sb, A_PAST, A_W), s3(sr["av"])], axis=1)
        bias_s = _rel_bias(lw["a_rel_bias"], t_new, A_PAST + t_new, A_PAST)
        bs["oa"] = _dense_attention(s3(sr["aq"]), ka, va, bias_s, t_new).reshape(sb * t_new, A_W)
        bs["ob"] = _diff_sample(s3(sr["bq"]), cache_bk, cache_bv, l, s3(sr["bkh"]), s3(sr["bvh"]),
                                lw["b_lambda"], lw["b_subln"], lam_init).reshape(sb * t_new, B_V_W)
        oc, c_new = _gla(s3(sr["cq"]), s3(sr["ck"]), s3(sr["cv"]), s3(sr["cg"]), s3(sr["cr"]),
                         state_c[l].reshape(sb, C_QK_W, C_DV), lw["c_on"], t_new)
        bs["oc"] = oc.reshape(sb * t_new, C_V_W)
        bs["om"] = _dense_attention(s3(sr["mq"]), cache_mem_k[l].reshape(sb, n_mem, M_W),
                                    cache_mem_v[l].reshape(sb, n_mem, M_W), None, t_new).reshape(sb * t_new, M_W)
        outs["a_ks"].append(sr["ak"].reshape(sb, t_new, A_HEADS, HEAD_DIM))
        outs["a_vs"].append(sr["av"].reshape(sb, t_new, A_HEADS, HEAD_DIM))
        outs["b_ks"].append(sr["bk"].reshape(sb, t_new, B_HEADS, 2, HEAD_DIM))
        outs["b_vs"].append(sr["bv"].reshape(sb, t_new, B_HEADS, B_VDIM))
        outs["c_ss"].append(c_new.reshape(sb, C_HEADS, C_DK, C_DV))
        xs = _channel_mixer(l, xs, bs, lw)

    st = {k: jnp.stack(v) for k, v in outs.items()}
    return (xp.reshape(nb, seq, d), xs.reshape(sb, t_new, d),
            st["a_kp"], st["a_vp"], st["a_ks"], st["a_vs"], st["b_kp"], st["b_vp"], st["b_ks"], st["b_vs"],
            st["c_sp"], st["c_ss"], st["m_kp"], st["m_vp"])
```
